```python
import math
import jax
import jax.numpy as jnp
from jax import lax
import numpy as np

D_MODEL = 2048
BATCH = 8
SEQ = 2048
DEPTH = 4
DEC_BATCH = 4
DEC_SEQ = 4096
PAST_LEN = 128

N_MIXERS = 4
Q_BLOCK = 128
NORM_EPS = 1e-6
F32 = jnp.float32

MLA_HEADS = 16
MLA_Q_RANK = 512
MLA_KV_RANK = 512
MLA_NOPE = 128
MLA_ROPE = 64
MLA_V = 128
MLA_THETA = 10000.0
MLA_IN = MLA_Q_RANK + MLA_KV_RANK + MLA_ROPE + MLA_HEADS * MLA_V

DIFF_HEADS = 16
DIFF_HD = D_MODEL // DIFF_HEADS // 2
DIFF_ROT = DIFF_HD // 4
ROPE_THETA = 500000.0
DIFF_QK = 2 * DIFF_HEADS * DIFF_HD
DIFF_VW = DIFF_HEADS * 2 * DIFF_HD

LRU_WIDTH = D_MODEL
LRU_BLOCKS = 16
LRU_BW = LRU_WIDTH // LRU_BLOCKS
LRU_CONV = 4
LRU_C = 8.0

RWKV_N = 64
RWKV_HEADS = D_MODEL // RWKV_N
RWKV_DECAY_LORA = 96
RWKV_A_LORA = 96
RWKV_GN_EPS = 64e-5

N_A = (DEPTH + 3) // 4
N_B = (DEPTH + 2) // 4
N_C = (DEPTH + 1) // 4
N_D = DEPTH // 4

kernel_name = 'hybrid_bidir_mla_diff_rglru_rwkv7_encoder'


def rms_norm(x, g, eps=NORM_EPS):
    xf = x.astype(F32)
    y = xf * lax.rsqrt(jnp.mean(xf * xf, axis=-1, keepdims=True) + eps)
    return (y * g.astype(F32)).astype(x.dtype)


def rope_tables(seq, dim, theta):
    inv = 1.0 / (theta ** (jnp.arange(0, dim, 2, dtype=F32) / dim))
    ang = jnp.arange(seq, dtype=F32)[:, None] * inv[None, :]
    ang = jnp.concatenate([ang, ang], axis=-1)
    return jnp.cos(ang), jnp.sin(ang)


def apply_rope(x, cos, sin):
    xf = x.astype(F32)
    half = x.shape[-1] // 2
    rot = jnp.concatenate([-xf[..., half:], xf[..., :half]], axis=-1)
    return (xf * cos[None, :, None, :] + rot * sin[None, :, None, :]).astype(x.dtype)


def blocked_attention(q, k, v, scale, mix_probs):
    b, s, hq, dq = q.shape
    nb = s // Q_BLOCK
    qb = jnp.moveaxis(q.reshape(b, nb, Q_BLOCK, hq, dq), 1, 0)
    kf = k.astype(F32)

    def one_block(qblk):
        sc = jnp.einsum('bqhd,bkhd->bhqk', qblk.astype(F32), kf) * scale
        p = mix_probs(jax.nn.softmax(sc, axis=-1))
        return jnp.einsum('bhqk,bkhd->bqhd', p.astype(v.dtype), v)

    out = jnp.moveaxis(lax.map(one_block, qb), 0, 1)
    return out.reshape(b, s, out.shape[3], out.shape[4])


def mla_mixer(h, w_in, q_norm_g, kv_norm_g, w_q_up, w_kv_up, w_out):
    b, s, _ = h.shape
    z = h @ w_in
    q_lat, kv_lat, k_pe, gate = jnp.split(
        z, [MLA_Q_RANK, MLA_Q_RANK + MLA_KV_RANK, MLA_Q_RANK + MLA_KV_RANK + MLA_ROPE], axis=-1)
    q = (rms_norm(q_lat, q_norm_g) @ w_q_up).reshape(b, s, MLA_HEADS, MLA_NOPE + MLA_ROPE)
    kv = (rms_norm(kv_lat, kv_norm_g) @ w_kv_up).reshape(b, s, MLA_HEADS, MLA_NOPE + MLA_V)
    k_nope, v = jnp.split(kv, [MLA_NOPE], axis=-1)
    cos, sin = rope_tables(s, MLA_ROPE, MLA_THETA)
    q_pe = apply_rope(q[..., MLA_NOPE:], cos, sin)
    k_pe = apply_rope(k_pe[:, :, None, :], cos, sin)
    q = jnp.concatenate([q[..., :MLA_NOPE], q_pe], axis=-1)
    k = jnp.concatenate([k_nope, jnp.broadcast_to(k_pe, (b, s, MLA_HEADS, MLA_ROPE))], axis=-1)
    o = blocked_attention(q, k, v, (MLA_NOPE + MLA_ROPE) ** -0.5, lambda p: p)
    o = o.reshape(b, s, MLA_HEADS * MLA_V) * jax.nn.silu(gate)
    return o @ w_out


def diff_mixer(h, layer_idx, w_in, lam, subln_g, w_out):
    b, s, _ = h.shape
    q, k, v, gate = jnp.split(h @ w_in, [DIFF_QK, 2 * DIFF_QK, 2 * DIFF_QK + DIFF_VW], axis=-1)
    q = q.reshape(b, s, 2 * DIFF_HEADS, DIFF_HD)
    k = k.reshape(b, s, 2 * DIFF_HEADS, DIFF_HD)
    v = v.reshape(b, s, DIFF_HEADS, 2 * DIFF_HD)
    cos, sin = rope_tables(s, DIFF_ROT, ROPE_THETA)
    q = jnp.concatenate([apply_rope(q[..., :DIFF_ROT], cos, sin), q[..., DIFF_ROT:]], axis=-1)
    k = jnp.concatenate([apply_rope(k[..., :DIFF_ROT], cos, sin), k[..., DIFF_ROT:]], axis=-1)
    lam_init = 0.8 - 0.6 * math.exp(-0.3 * layer_idx)
    lf = lam.astype(F32)
    lam_full = jnp.exp(jnp.sum(lf[0] * lf[1])) - jnp.exp(jnp.sum(lf[2] * lf[3])) + lam_init

    def diff_probs(p):
        p = p.reshape(p.shape[0], DIFF_HEADS, 2, p.shape[2], p.shape[3])
        return p[:, :, 0] - lam_full * p[:, :, 1]

    o = blocked_attention(q, k, v, DIFF_HD ** -0.5, diff_probs)
    o = rms_norm(o, subln_g, eps=1e-5) * (1.0 - lam_init)
    o = o.reshape(b, s, DIFF_VW) * jax.nn.silu(gate)
    return o @ w_out


def rglru_scan(xc, gate_w, gate_b, lam, reverse):
    b, s, w = xc.shape
    xr = xc.reshape(b, s, LRU_BLOCKS, LRU_BW)
    gates = jnp.einsum('bsnh,gnhk->gbsnk', xr, gate_w.astype(F32)).reshape(2, b, s, w)
    gates = gates + gate_b.astype(F32)[:, None, None, :]
    r_t = jax.nn.sigmoid(gates[0])
    i_t = jax.nn.sigmoid(gates[1])
    log_a = -LRU_C * r_t * jax.nn.softplus(-lam.astype(F32))
    a_t = jnp.exp(log_a)
    mult = jnp.sqrt(-jnp.expm1(2.0 * log_a))
    first = s - 1 if reverse else 0
    mult = jnp.where((jnp.arange(s) == first)[None, :, None], 1.0, mult)
    u = mult * i_t * xc

    def combine(e1, e2):
        a1, b1 = e1
        a2, b2 = e2
        return a1 * a2, a2 * b1 + b2

    _, hs = lax.associative_scan(combine, (a_t, u), reverse=reverse, axis=1)
    return hs


def lru_mixer(h, w_in, conv_w, conv_b, gate_w, gate_b, lam, w_out):
    xb, gate = jnp.split(h @ w_in, [LRU_WIDTH], axis=-1)
    lo = (LRU_CONV - 1) // 2
    hi = LRU_CONV - 1 - lo
    xc = lax.conv_general_dilated(xb, conv_w[:, None, :], window_strides=(1,), padding=[(lo, hi)],
                                  dimension_numbers=('NWC', 'WIO', 'NWC'),
                                  feature_group_count=LRU_WIDTH) + conv_b
    xf = xc.astype(F32)
    y = (rglru_scan(xf, gate_w[0], gate_b[0], lam[0], False)
         + rglru_scan(xf, gate_w[1], gate_b[1], lam[1], True))
    return (y.astype(h.dtype) * jax.nn.silu(gate)) @ w_out


def rwkv_scan(decay, kneg, kb, v, k, r, reverse):
    b, s, hh, n = v.shape

    def step(state, inp):
        w_t, ka_t, kb_t, v_t, k_t, r_t = inp
        sa = jnp.einsum('bhij,bhj->bhi', state, ka_t)
        state = (state * w_t[:, :, None, :] + sa[..., None] * kb_t[:, :, None, :]
                 + v_t[..., None] * k_t[:, :, None, :])
        return state, jnp.einsum('bhij,bhj->bhi', state, r_t)

    xs = tuple(jnp.moveaxis(t, 1, 0) for t in (decay, kneg, kb, v, k, r))
    _, out = lax.scan(step, jnp.zeros((b, hh, n, n), F32), xs, reverse=reverse)
    return jnp.moveaxis(out, 0, 1)


def rwkv_mixer(h, mu, w_in, w0, w1, w2, a0, a1, a2, k_k, k_a, r_k, ln_g, ln_b, w_out):
    b, s, d = h.shape

    def heads(t):
        return t.reshape(t.shape[0], t.shape[1], RWKV_HEADS, RWKV_N)

    zero = jnp.zeros_like(h[:, :1])
    x_prev = jnp.concatenate([zero, h[:, :-1]], axis=1)
    x_next = jnp.concatenate([h[:, 1:], zero], axis=1)
    xx = 0.5 * (x_prev + x_next) - h
    xs = h[:, :, None, :] + xx[:, :, None, :] * mu
    rkvg = jnp.einsum('bsmd,mde->bsme', xs[:, :, :4], w_in)
    r = heads(rkvg[:, :, 0].astype(F32))
    k = heads(rkvg[:, :, 1].astype(F32))
    v = heads(rkvg[:, :, 2].astype(F32))
    g = rkvg[:, :, 3]
    xw = xs[:, :, 4]
    xa = xs[:, :, 5]
    kk = k * k_k.astype(F32).reshape(RWKV_HEADS, RWKV_N)
    kk = kk / jnp.maximum(jnp.sqrt(jnp.sum(kk * kk, axis=-1, keepdims=True)), 1e-12)
    k_af = k_a.astype(F32).reshape(RWKV_HEADS, RWKV_N)
    r_kf = r_k.astype(F32)

    def direction(dr):
        wl = -jax.nn.softplus(-(w0[dr] + jnp.tanh(xw @ w1[dr]) @ w2[dr]).astype(F32)) - 0.5
        decay = heads(jnp.exp(-jnp.exp(wl)))
        a = heads(jax.nn.sigmoid((a0[dr] + (xa @ a1[dr]) @ a2[dr]).astype(F32)))
        kd = k * (1.0 + (a - 1.0) * k_af)
        o = rwkv_scan(decay, -kk, kk * a, v, kd, r, reverse=(dr == 1))
        bonus = jnp.sum(r * kd * r_kf, axis=-1, keepdims=True) * v
        return o, bonus

    o_f, bonus_f = direction(0)
    o_b, bonus_b = direction(1)
    o = o_f + o_b
    mean = jnp.mean(o, axis=-1, keepdims=True)
    var = jnp.mean(jnp.square(o - mean), axis=-1, keepdims=True)
    o = (o - mean) * lax.rsqrt(var + RWKV_GN_EPS)
    o = o.reshape(b, s, d) * ln_g.astype(F32) + ln_b.astype(F32) + (bonus_f + bonus_b).reshape(b, s, d)
    o = o.astype(h.dtype) * jax.nn.silu(g)
    return o @ w_out


def run_trunk(x, c, p):
    cs = jax.nn.silu(c)
    for i in range(DEPTH):
        mod = cs @ p['ada_w'][i] + p['ada_b'][i]
        shift, scale, gate = jnp.split(mod[:, None, :], 3, axis=-1)
        hdn = rms_norm(x, p['norm_pre_g'][i]) * (1.0 + scale) + shift
        m, j = i % N_MIXERS, i // N_MIXERS
        if m == 0:
            y = mla_mixer(hdn, p['mla_w_in'][j], p['mla_q_norm_g'][j], p['mla_kv_norm_g'][j],
                          p['mla_w_q_up'][j], p['mla_w_kv_up'][j], p['mla_w_out'][j])
        elif m == 1:
            y = diff_mixer(hdn, i, p['diff_w_in'][j], p['diff_lambda'][j], p['diff_subln_g'][j],
                           p['diff_w_out'][j])
        elif m == 2:
            y = lru_mixer(hdn, p['lru_w_in'][j], p['lru_conv_w'][j], p['lru_conv_b'][j],
                          p['lru_gate_w'][j], p['lru_gate_b'][j], p['lru_lambda'][j], p['lru_w_out'][j])
        else:
            y = rwkv_mixer(hdn, p['rwkv_mu'][j], p['rwkv_w_in'][j], p['rwkv_w0'][j], p['rwkv_w1'][j],
                           p['rwkv_w2'][j], p['rwkv_a0'][j], p['rwkv_a1'][j], p['rwkv_a2'][j],
                           p['rwkv_k_k'][j], p['rwkv_k_a'][j], p['rwkv_r_k'][j], p['rwkv_ln_g'][j],
                           p['rwkv_ln_b'][j], p['rwkv_w_out'][j])
        x = x + gate * rms_norm(y, p['norm_post_g'][i])
    return x


def setup_inputs(seed: int = 0) -> dict:
    key = jax.random.key(seed)
    ks = iter(jax.random.split(key, 64))
    D = D_MODEL

    def nrm(shape, scale):
        return scale * jax.random.normal(next(ks), shape, F32)

    def gain(shape):
        return 1.0 + nrm(shape, 0.02)

    u = jax.random.uniform(next(ks), (N_C, 2, LRU_WIDTH), F32, 0.9, 0.999)
    a_base = u ** (1.0 / LRU_C)
    lru_lambda = jnp.log(a_base) - jnp.log1p(-a_base)
    return {
        'x_prompt': nrm((BATCH, SEQ, D), 1.0),
        'x_sample': nrm((DEC_BATCH, DEC_SEQ, D), 1.0),
        'c_prompt': nrm((BATCH, D), 1.0),
        'c_sample': nrm((DEC_BATCH, D), 1.0),
        'ada_w': nrm((DEPTH, D, 3 * D), D ** -0.5),
        'ada_b': nrm((DEPTH, 3 * D), 0.02),
        'norm_pre_g': gain((DEPTH, D)),
        'norm_post_g': gain((DEPTH, D)),
        'mla_w_in': nrm((N_A, D, MLA_IN), D ** -0.5),
        'mla_q_norm_g': gain((N_A, MLA_Q_RANK)),
        'mla_kv_norm_g': gain((N_A, MLA_KV_RANK)),
        'mla_w_q_up': nrm((N_A, MLA_Q_RANK, MLA_HEADS * (MLA_NOPE + MLA_ROPE)), MLA_Q_RANK ** -0.5),
        'mla_w_kv_up': nrm((N_A, MLA_KV_RANK, MLA_HEADS * (MLA_NOPE + MLA_V)), MLA_KV_RANK ** -0.5),
        'mla_w_out': nrm((N_A, MLA_HEADS * MLA_V, D), (MLA_HEADS * MLA_V) ** -0.5),
        'diff_w_in': nrm((N_B, D, 2 * DIFF_QK + 2 * DIFF_VW), D ** -0.5),
        'diff_lambda': nrm((N_B, 4, DIFF_HD), 0.1),
        'diff_subln_g': gain((N_B, 2 * DIFF_HD)),
        'diff_w_out': nrm((N_B, DIFF_VW, D), DIFF_VW ** -0.5),
        'lru_w_in': nrm((N_C, D, 2 * LRU_WIDTH), D ** -0.5),
        'lru_conv_w': nrm((N_C, LRU_CONV, LRU_WIDTH), LRU_CONV ** -0.5),
        'lru_conv_b': nrm((N_C, LRU_WIDTH), 0.02),
        'lru_gate_w': nrm((N_C, 2, 2, LRU_BLOCKS, LRU_BW, LRU_BW), LRU_BW ** -0.5),
        'lru_gate_b': nrm((N_C, 2, 2, LRU_WIDTH), 0.02),
        'lru_lambda': lru_lambda,
        'lru_w_out': nrm((N_C, LRU_WIDTH, D), LRU_WIDTH ** -0.5),
        'rwkv_mu': jax.random.uniform(next(ks), (N_D, 6, D), F32),
        'rwkv_w_in': nrm((N_D, 4, D, D), D ** -0.5),
        'rwkv_w0': jax.random.uniform(next(ks), (N_D, 2, D), F32, -6.0, -1.0),
        'rwkv_w1': nrm((N_D, 2, D, RWKV_DECAY_LORA), D ** -0.5),
        'rwkv_w2': nrm((N_D, 2, RWKV_DECAY_LORA, D), 0.1 * RWKV_DECAY_LORA ** -0.5),
        'rwkv_a0': nrm((N_D, 2, D), 0.1),
        'rwkv_a1': nrm((N_D, 2, D, RWKV_A_LORA), D ** -0.5),
        'rwkv_a2': nrm((N_D, 2, RWKV_A_LORA, D), 0.1 * RWKV_A_LORA ** -0.5),
        'rwkv_k_k': 0.85 + nrm((N_D, D), 0.02),
        'rwkv_k_a': gain((N_D, D)),
        'rwkv_r_k': -0.04 + nrm((N_D, RWKV_HEADS, RWKV_N), 0.05),
        'rwkv_ln_g': gain((N_D, D)),
        'rwkv_ln_b': nrm((N_D, D), 0.02),
        'rwkv_w_out': nrm((N_D, D, D), D ** -0.5),
    }


def reference(x_prompt, x_sample, c_prompt, c_sample, ada_w, ada_b, norm_pre_g, norm_post_g,
              mla_w_in, mla_q_norm_g, mla_kv_norm_g, mla_w_q_up, mla_w_kv_up, mla_w_out,
              diff_w_in, diff_lambda, diff_subln_g, diff_w_out,
              lru_w_in, lru_conv_w, lru_conv_b, lru_gate_w, lru_gate_b, lru_lambda, lru_w_out,
              rwkv_mu, rwkv_w_in, rwkv_w0, rwkv_w1, rwkv_w2, rwkv_a0, rwkv_a1, rwkv_a2,
              rwkv_k_k, rwkv_k_a, rwkv_r_k, rwkv_ln_g, rwkv_ln_b, rwkv_w_out):
    params = {
        'ada_w': ada_w, 'ada_b': ada_b, 'norm_pre_g': norm_pre_g, 'norm_post_g': norm_post_g,
        'mla_w_in': mla_w_in, 'mla_q_norm_g': mla_q_norm_g, 'mla_kv_norm_g': mla_kv_norm_g,
        'mla_w_q_up': mla_w_q_up, 'mla_w_kv_up': mla_w_kv_up, 'mla_w_out': mla_w_out,
        'diff_w_in': diff_w_in, 'diff_lambda': diff_lambda, 'diff_subln_g': diff_subln_g,
        'diff_w_out': diff_w_out,
        'lru_w_in': lru_w_in, 'lru_conv_w': lru_conv_w, 'lru_conv_b': lru_conv_b,
        'lru_gate_w': lru_gate_w, 'lru_gate_b': lru_gate_b, 'lru_lambda': lru_lambda,
        'lru_w_out': lru_w_out,
        'rwkv_mu': rwkv_mu, 'rwkv_w_in': rwkv_w_in, 'rwkv_w0': rwkv_w0, 'rwkv_w1': rwkv_w1,
        'rwkv_w2': rwkv_w2, 'rwkv_a0': rwkv_a0, 'rwkv_a1': rwkv_a1, 'rwkv_a2': rwkv_a2,
        'rwkv_k_k': rwkv_k_k, 'rwkv_k_a': rwkv_k_a, 'rwkv_r_k': rwkv_r_k, 'rwkv_ln_g': rwkv_ln_g,
        'rwkv_ln_b': rwkv_ln_b, 'rwkv_w_out': rwkv_w_out,
    }
    y_prompt = run_trunk(x_prompt, c_prompt, params)
    y_sample = run_trunk(x_sample, c_sample, params)
    return (y_prompt, y_sample)
```

```python
import functools
import math

import jax
import jax.numpy as jnp
from jax import lax
from jax.experimental import pallas as pl
from jax.experimental.pallas import tpu as pltpu

F32 = jnp.float32
BF16 = jnp.bfloat16

D_MODEL = 2048
DEPTH = 4
NORM_EPS = 1e-6
LANES = 128
VMEM_LIMIT = 56 * 1024 * 1024

MLA_HEADS = 16
MLA_Q_RANK = 512
MLA_KV_RANK = 512
MLA_NOPE = 128
MLA_ROPE = 64
MLA_V = 128
MLA_THETA = 10000.0
MLA_QK_PAD = 256

DIFF_HEADS = 16
DIFF_HD = 64
DIFF_ROT = 16
ROPE_THETA = 500000.0
DIFF_SUBLN_EPS = 1e-5

LRU_WIDTH = 2048
LRU_BLOCKS = 16
LRU_BW = 128
LRU_C = 8.0
LRU_HALO = 16

RWKV_N = 64
RWKV_HEADS = 32
RWKV_LORA = 96
RWKV_LORA_PAD = 128
RWKV_GN_EPS = 64e-5
RWKV_CHUNK = 64


def _params(*sem):
    return pltpu.CompilerParams(dimension_semantics=sem, vmem_limit_bytes=VMEM_LIMIT)


def _silu(x):
    return x * jax.nn.sigmoid(x)


def _rope128(y, c, s1, s2, shift):
    return y * c + pltpu.roll(y, LANES - shift, 1) * s1 + pltpu.roll(y, shift, 1) * s2


def _rope_tables(seq, heads_per_chunk, head_dim, rot_dim, theta, pad_cos):
    half = rot_dim // 2
    inv = 1.0 / (theta ** (jnp.arange(0, rot_dim, 2, dtype=F32) / rot_dim))
    ang = jnp.arange(seq, dtype=F32)[:, None] * inv[None, :]
    cos, sin = jnp.cos(ang), jnp.sin(ang)
    rest = head_dim - rot_dim
    c = jnp.concatenate([cos, cos, jnp.full((seq, rest), pad_cos, F32)], axis=-1)
    s1 = jnp.concatenate([-sin, jnp.zeros((seq, half + rest), F32)], axis=-1)
    s2 = jnp.concatenate([jnp.zeros((seq, half), F32), sin, jnp.zeros((seq, rest), F32)], axis=-1)
    tile = lambda t: jnp.tile(t, (1, heads_per_chunk))
    return tile(c), tile(s1), tile(s2)


def _ada_kernel(c_ref, w_ref, b_ref, o_ref):
    cs = _silu(c_ref[...]).astype(BF16)
    o_ref[0] = jnp.dot(cs, w_ref[0].astype(BF16), preferred_element_type=F32) + b_ref[0]


def ada_mod(c, ada_w, ada_b):
    nb, d = c.shape
    depth, _, n = ada_w.shape
    tn = 512
    return pl.pallas_call(
        _ada_kernel,
        grid=(depth, n // tn),
        in_specs=[
            pl.BlockSpec((nb, d), lambda l, j: (0, 0)),
            pl.BlockSpec((1, d, tn), lambda l, j: (l, 0, j)),
            pl.BlockSpec((1, 1, tn), lambda l, j: (l, 0, j)),
        ],
        out_specs=pl.BlockSpec((1, nb, tn), lambda l, j: (l, 0, j)),
        out_shape=jax.ShapeDtypeStruct((depth, nb, n), F32),
        compiler_params=_params("parallel", "parallel"),
        name="ada_mod",
    )(c, ada_w, ada_b.reshape(depth, 1, n))


def _modnorm(x, g, sc, sh):
    ms = jnp.mean(x * x, axis=-1, keepdims=True)
    return x * lax.rsqrt(ms + NORM_EPS) * g * (1.0 + sc) + sh


def _store_with_rope(y, o_ref, j, tn, rope, tabs):
    if rope is None:
        o_ref[0] = y.astype(o_ref.dtype)
        return
    lo, hi, shift, stride, phase = rope
    c_ref, s1_ref, s2_ref = tabs
    cpt = tn // LANES
    for c in range(cpt):
        gc = j * cpt + c
        yc = y[:, c * LANES:(c + 1) * LANES]
        is_rope = jnp.logical_and(jnp.logical_and(gc >= lo, gc < hi), gc % stride == phase)

        @pl.when(is_rope)
        def _(yc=yc, c=c):
            o_ref[0, :, c * LANES:(c + 1) * LANES] = _rope128(
                yc, c_ref[...], s1_ref[...], s2_ref[...], shift).astype(o_ref.dtype)

        @pl.when(jnp.logical_not(is_rope))
        def _(yc=yc, c=c):
            o_ref[0, :, c * LANES:(c + 1) * LANES] = yc.astype(o_ref.dtype)


def _modmm_kernel(x_ref, g_ref, sc_ref, sh_ref, w_ref, *rest, tn, rope):
    tabs, (o_ref, h_ref) = rest[:-2], rest[-2:]
    j = pl.program_id(2)

    @pl.when(j == 0)
    def _():
        h_ref[...] = _modnorm(x_ref[0], g_ref[...], sc_ref[0], sh_ref[0]).astype(BF16)

    y = jnp.dot(h_ref[...], w_ref[...], preferred_element_type=F32)
    _store_with_rope(y, o_ref, j, tn, rope, tabs)


def modmm(x, g, scale, shift, w, *, ts, tn, out_dtype=BF16, rope=None, tabs=()):
    b, s, d = x.shape
    n = w.shape[1]
    tab_specs = [pl.BlockSpec((ts, LANES), lambda bi, i, j: (i, 0)) for _ in tabs]
    return pl.pallas_call(
        functools.partial(_modmm_kernel, tn=tn, rope=rope),
        grid=(b, s // ts, n // tn),
        in_specs=[
            pl.BlockSpec((1, ts, d), lambda bi, i, j: (bi, i, 0)),
            pl.BlockSpec((1, d), lambda bi, i, j: (0, 0)),
            pl.BlockSpec((1, 1, d), lambda bi, i, j: (bi, 0, 0)),
            pl.BlockSpec((1, 1, d), lambda bi, i, j: (bi, 0, 0)),
            pl.BlockSpec((d, tn), lambda bi, i, j: (0, j)),
        ] + tab_specs,
        out_specs=pl.BlockSpec((1, ts, tn), lambda bi, i, j: (bi, i, j)),
        out_shape=jax.ShapeDtypeStruct((b, s, n), out_dtype),
        scratch_shapes=[pltpu.VMEM((ts, d), BF16)],
        compiler_params=_params("parallel", "parallel", "arbitrary"),
        name="modmm",
    )(x, g, scale, shift, w, *tabs)


def _normmm_kernel(x_ref, g_ref, w_ref, *rest, tn, rope):
    tabs, (o_ref, h_ref) = rest[:-2], rest[-2:]
    j = pl.program_id(2)

    @pl.when(j == 0)
    def _():
        x = x_ref[0].astype(F32)
        ms = jnp.mean(x * x, axis=-1, keepdims=True)
        h_ref[...] = (x * lax.rsqrt(ms + NORM_EPS) * g_ref[...]).astype(BF16)

    y = jnp.dot(h_ref[...], w_ref[...], preferred_element_type=F32)
    _store_with_rope(y, o_ref, j, tn, rope, tabs)


def normmm(z, xcol, g, w, *, ts, tn, rope=None, tabs=()):
    b, s, _ = z.shape
    k, n = w.shape
    tab_specs = [pl.BlockSpec((ts, LANES), lambda bi, i, j: (i, 0)) for _ in tabs]
    return pl.pallas_call(
        functools.partial(_normmm_kernel, tn=tn, rope=rope),
        grid=(b, s // ts, n // tn),
        in_specs=[
            pl.BlockSpec((1, ts, k), lambda bi, i, j: (bi, i, xcol)),
            pl.BlockSpec((1, k), lambda bi, i, j: (0, 0)),
            pl.BlockSpec((k, tn), lambda bi, i, j: (0, j)),
        ] + tab_specs,
        out_specs=pl.BlockSpec((1, ts, tn), lambda bi, i, j: (bi, i, j)),
        out_shape=jax.ShapeDtypeStruct((b, s, n), BF16),
        scratch_shapes=[pltpu.VMEM((ts, k), BF16)],
        compiler_params=_params("parallel", "parallel", "arbitrary"),
        name="normmm",
    )(z, g, w, *tabs)


def _flash(q, k_ref, v_ref, m_sc, l_sc, acc_sc, tk):
    seq = k_ref.shape[0]
    m_sc[...] = jnp.full(m_sc.shape, -jnp.inf, F32)
    l_sc[...] = jnp.zeros(l_sc.shape, F32)
    acc_sc[...] = jnp.zeros(acc_sc.shape, F32)

    def body(c, carry):
        off = pl.multiple_of(c * tk, tk)
        k = k_ref[pl.ds(off, tk), :]
        s = lax.dot_general(q, k, (((1,), (1,)), ((), ())), preferred_element_type=F32)
        m_prev = m_sc[...]
        m_new = jnp.maximum(m_prev, jnp.max(s, axis=-1, keepdims=True))
        alpha = jnp.exp(m_prev - m_new)
        p = jnp.exp(s - m_new)
        l_sc[...] = alpha * l_sc[...] + jnp.sum(p, axis=-1, keepdims=True)
        acc_sc[...] = alpha * acc_sc[...] + jnp.dot(
            p.astype(BF16), v_ref[pl.ds(off, tk), :], preferred_element_type=F32)
        m_sc[...] = m_new
        return carry

    lax.fori_loop(0, seq // tk, body, 0)


def _mla_attn_kernel(q_ref, kn_ref, v_ref, kpe_ref, o_ref, k_sc, m_sc, l_sc, acc_sc, *, tk, scale):
    @pl.when(pl.program_id(2) == 0)
    def _():
        k_sc[:, :LANES] = kn_ref[0]
        k_sc[:, LANES:] = kpe_ref[0]

    q = (q_ref[0].astype(F32) * scale).astype(BF16)
    _flash(q, k_sc, v_ref.at[0], m_sc, l_sc, acc_sc, tk)
    o_ref[0] = (acc_sc[...] / l_sc[...]).astype(o_ref.dtype)


def mla_attention(q, kv, z, kpe_col, *, tq, tk):
    b, s, _ = q.shape
    scale = (MLA_NOPE + MLA_ROPE) ** -0.5
    return pl.pallas_call(
        functools.partial(_mla_attn_kernel, tk=tk, scale=scale),
        grid=(b, MLA_HEADS, s // tq),
        in_specs=[
            pl.BlockSpec((1, tq, MLA_QK_PAD), lambda bi, h, i: (bi, i, h)),
            pl.BlockSpec((1, s, LANES), lambda bi, h, i: (bi, 0, 2 * h)),
            pl.BlockSpec((1, s, LANES), lambda bi, h, i: (bi, 0, 2 * h + 1)),
            pl.BlockSpec((1, s, LANES), lambda bi, h, i: (bi, 0, kpe_col)),
        ],
        out_specs=pl.BlockSpec((1, tq, MLA_V), lambda bi, h, i: (bi, i, h)),
        out_shape=jax.ShapeDtypeStruct((b, s, MLA_HEADS * MLA_V), BF16),
        scratch_shapes=[
            pltpu.VMEM((s, MLA_QK_PAD), BF16),
            pltpu.VMEM((tq, 1), F32),
            pltpu.VMEM((tq, 1), F32),
            pltpu.VMEM((tq, MLA_V), F32),
        ],
        compiler_params=_params("parallel", "parallel", "arbitrary"),
        name="mla_attention",
    )(q, kv, kv, z)


def _diff_attn_kernel(q_ref, k_ref, v_ref, lam_ref, g_ref, o_ref, m_sc, l_sc, acc_sc,
                      *, tq, tk, scale, lam_init):
    q = q_ref[0].astype(F32) * scale
    lane = lax.broadcasted_iota(jnp.int32, q.shape, 1)
    q1 = jnp.where(lane < DIFF_HD, q, 0.0).astype(BF16)
    q2 = jnp.where(lane >= DIFF_HD, q, 0.0).astype(BF16)
    _flash(jnp.concatenate([q1, q2], axis=0), k_ref.at[0], v_ref.at[0], m_sc, l_sc, acc_sc, tk)
    lam = lam_ref[...]
    lam_full = (jnp.exp(jnp.sum(lam[0:1] * lam[1:2], axis=-1, keepdims=True))
                - jnp.exp(jnp.sum(lam[2:3] * lam[3:4], axis=-1, keepdims=True)) + lam_init)
    o1 = acc_sc[:tq, :] / l_sc[:tq, :]
    o2 = acc_sc[tq:, :] / l_sc[tq:, :]
    o = o1 - lam_full * o2
    ms = jnp.mean(o * o, axis=-1, keepdims=True)
    o = o * lax.rsqrt(ms + DIFF_SUBLN_EPS) * g_ref[...] * (1.0 - lam_init)
    o_ref[0] = o.astype(o_ref.dtype)


def diff_attention(z, lam, subln_g, lam_init, *, tq, tk):
    b, s, _ = z.shape
    scale = DIFF_HD ** -0.5
    nh = DIFF_HEADS
    return pl.pallas_call(
        functools.partial(_diff_attn_kernel, tq=tq, tk=tk, scale=scale, lam_init=lam_init),
        grid=(b, nh, s // tq),
        in_specs=[
            pl.BlockSpec((1, tq, LANES), lambda bi, h, i: (bi, i, h)),
            pl.BlockSpec((1, s, LANES), lambda bi, h, i: (bi, 0, nh + h)),
            pl.BlockSpec((1, s, LANES), lambda bi, h, i: (bi, 0, 2 * nh + h)),
            pl.BlockSpec((4, DIFF_HD), lambda bi, h, i: (0, 0)),
            pl.BlockSpec((1, LANES), lambda bi, h, i: (0, 0)),
        ],
        out_specs=pl.BlockSpec((1, tq, LANES), lambda bi, h, i: (bi, i, h)),
        out_shape=jax.ShapeDtypeStruct((b, s, nh * LANES), BF16),
        scratch_shapes=[
            pltpu.VMEM((2 * tq, 1), F32),
            pltpu.VMEM((2 * tq, 1), F32),
            pltpu.VMEM((2 * tq, LANES), F32),
        ],
        compiler_params=_params("parallel", "parallel", "arbitrary"),
        name="diff_attention",
    )(z, z, z, lam, subln_g.reshape(1, LANES))


def _finish(a, w_ref, x_ref, gm_ref, gp_ref, o_ref):
    y = jnp.dot(a.astype(BF16), w_ref[...], preferred_element_type=F32)
    ms = jnp.mean(y * y, axis=-1, keepdims=True)
    o_ref[0] = x_ref[0] + gm_ref[0] * (y * lax.rsqrt(ms + NORM_EPS) * gp_ref[...])


def _outmm_kernel(a_ref, g_ref, w_ref, x_ref, gm_ref, gp_ref, o_ref):
    a = a_ref[0].astype(F32) * _silu(g_ref[0].astype(F32))
    _finish(a, w_ref, x_ref, gm_ref, gp_ref, o_ref)


def _outmm2_kernel(a1_ref, a2_ref, g_ref, w_ref, x_ref, gm_ref, gp_ref, o_ref):
    a = (a1_ref[0] + a2_ref[0]) * _silu(g_ref[0].astype(F32))
    _finish(a, w_ref, x_ref, gm_ref, gp_ref, o_ref)


def outmm(acts, zgate, gcol, w, x, gate_mod, g_post, *, ts):
    b, s, d = x.shape
    row = lambda bi, i: (bi, i, 0)
    kern = _outmm_kernel if len(acts) == 1 else _outmm2_kernel
    return pl.pallas_call(
        kern,
        grid=(b, s // ts),
        in_specs=[pl.BlockSpec((1, ts, d), row) for _ in acts] + [
            pl.BlockSpec((1, ts, d), lambda bi, i: (bi, i, gcol)),
            pl.BlockSpec((d, d), lambda bi, i: (0, 0)),
            pl.BlockSpec((1, ts, d), row),
            pl.BlockSpec((1, 1, d), lambda bi, i: (bi, 0, 0)),
            pl.BlockSpec((1, d), lambda bi, i: (0, 0)),
        ],
        out_specs=pl.BlockSpec((1, ts, d), row),
        out_shape=jax.ShapeDtypeStruct((b, s, d), F32),
        compiler_params=_params("parallel", "parallel"),
        name="outmm",
    )(*acts, zgate, w, x, gate_mod, g_post)


def _lru_prepare(xe_sc, x_ref, prev_ref, next_ref, is_first, is_last, cw_ref, cb_ref, gw_ref,
                 gb_ref, sp_ref, a_sc, u_sc, d, ts, edge_row):
    w = LRU_WIDTH
    h = LRU_HALO
    xe_sc[pl.ds(h, ts), :] = x_ref[0].astype(F32)

    @pl.when(is_first)
    def _():
        xe_sc[pl.ds(0, h), :] = jnp.zeros((h, w), F32)

    @pl.when(jnp.logical_not(is_first))
    def _():
        xe_sc[pl.ds(0, h), :] = prev_ref[0].astype(F32)

    @pl.when(is_last)
    def _():
        xe_sc[pl.ds(h + ts, h), :] = jnp.zeros((h, w), F32)

    @pl.when(jnp.logical_not(is_last))
    def _():
        xe_sc[pl.ds(h + ts, h), :] = next_ref[0].astype(F32)

    xc = cb_ref[...] + sum(cw_ref[i:i + 1, :] * xe_sc[pl.ds(h - 1 + i, ts), :] for i in range(4))
    row = lax.broadcasted_iota(jnp.int32, (ts, LRU_BW), 0)
    for n in range(LRU_BLOCKS):
        cols = slice(n * LRU_BW, (n + 1) * LRU_BW)
        xn = xc[:, cols]
        g = jnp.dot(xn.astype(BF16), gw_ref[d, n], preferred_element_type=F32)
        r_t = jax.nn.sigmoid(g[:, :LRU_BW] + gb_ref[2 * d:2 * d + 1, cols])
        i_t = jax.nn.sigmoid(g[:, LRU_BW:] + gb_ref[2 * d + 1:2 * d + 2, cols])
        log_a = -LRU_C * r_t * sp_ref[d:d + 1, cols]
        a_t = jnp.exp(log_a)
        mult = jnp.sqrt(-jnp.tanh(log_a) * (1.0 + a_t * a_t))
        mult = jnp.where(row == edge_row, 1.0, mult)
        a_sc[:, cols] = a_t
        u_sc[:, cols] = mult * i_t * xn


def _lru_kernel(xf_ref, pf_ref, nf_ref, xb_ref, pb_ref, nb_ref, cw_ref, cb_ref, gw_ref, gb_ref,
                lam_ref, yf_ref, yb_ref, xe_sc, af_sc, uf_sc, ab_sc, ub_sc, hf_sc, hb_sc, sp_sc,
                *, ts):
    i = pl.program_id(1)
    nt = pl.num_programs(1)

    @pl.when(i == 0)
    def _():
        hf_sc[...] = jnp.zeros(hf_sc.shape, F32)
        hb_sc[...] = jnp.zeros(hb_sc.shape, F32)
        nl = -lam_ref[...]
        sp_sc[...] = jnp.maximum(nl, 0.0) + jnp.log(1.0 + jnp.exp(-jnp.abs(nl)))

    first, last = i == 0, i == nt - 1
    _lru_prepare(xe_sc, xf_ref, pf_ref, nf_ref, first, last, cw_ref, cb_ref, gw_ref, gb_ref,
                 sp_sc, af_sc, uf_sc, 0, ts, jnp.where(first, 0, -1))
    _lru_prepare(xe_sc, xb_ref, pb_ref, nb_ref, last, first, cw_ref, cb_ref, gw_ref, gb_ref,
                 sp_sc, ab_sc, ub_sc, 1, ts, jnp.where(first, ts - 1, -1))

    def step(k, carry):
        hf, hb = carry
        kb = ts - 1 - k
        hf = af_sc[pl.ds(k, 1), :] * hf + uf_sc[pl.ds(k, 1), :]
        hb = ab_sc[pl.ds(kb, 1), :] * hb + ub_sc[pl.ds(kb, 1), :]
        yf_ref[0, pl.ds(k, 1), :] = hf
        yb_ref[0, pl.ds(kb, 1), :] = hb
        return hf, hb

    hf, hb = lax.fori_loop(0, ts, step, (hf_sc[...], hb_sc[...]), unroll=8)
    hf_sc[...] = hf
    hb_sc[...] = hb


def lru_scan(z, conv_w, conv_b, gate_w, gate_b, lam, *, ts):
    b, s, _ = z.shape
    w = LRU_WIDTH
    h = LRU_HALO
    nt = s // ts
    rpb = ts // h
    nhb = s // h
    fwd = lambda bi, i: (bi, i, 0)
    bwd = lambda bi, i: (bi, nt - 1 - i, 0)
    fwd_prev = lambda bi, i: (bi, jnp.maximum(i * rpb - 1, 0), 0)
    fwd_next = lambda bi, i: (bi, jnp.minimum((i + 1) * rpb, nhb - 1), 0)
    bwd_prev = lambda bi, i: (bi, jnp.maximum((nt - 1 - i) * rpb - 1, 0), 0)
    bwd_next = lambda bi, i: (bi, jnp.minimum((nt - i) * rpb, nhb - 1), 0)
    full = lambda *shape: pl.BlockSpec(shape, lambda bi, i: (0,) * len(shape))
    return pl.pallas_call(
        functools.partial(_lru_kernel, ts=ts),
        grid=(b, nt),
        in_specs=[
            pl.BlockSpec((1, ts, w), fwd), pl.BlockSpec((1, h, w), fwd_prev),
            pl.BlockSpec((1, h, w), fwd_next),
            pl.BlockSpec((1, ts, w), bwd), pl.BlockSpec((1, h, w), bwd_prev),
            pl.BlockSpec((1, h, w), bwd_next),
            full(4, w), full(1, w), full(2, LRU_BLOCKS, LRU_BW, 2 * LRU_BW), full(4, w), full(2, w),
        ],
        out_specs=[pl.BlockSpec((1, ts, w), fwd), pl.BlockSpec((1, ts, w), bwd)],
        out_shape=[jax.ShapeDtypeStruct((b, s, w), F32)] * 2,
        scratch_shapes=[
            pltpu.VMEM((ts + 2 * h, w), F32),
            pltpu.VMEM((ts, w), F32), pltpu.VMEM((ts, w), F32),
            pltpu.VMEM((ts, w), F32), pltpu.VMEM((ts, w), F32),
            pltpu.VMEM((1, w), F32), pltpu.VMEM((1, w), F32), pltpu.VMEM((2, w), F32),
        ],
        compiler_params=_params("parallel", "arbitrary"),
        name="lru_scan",
    )(z, z, z, z, z, z, conv_w, conv_b, gate_w, gate_b, lam)


RWKV_HALO = 8


def _rwkv_shifted(x_ref, p_ref, n_ref, g_ref, sc_ref, sh_ref, he_sc, is_first, is_last, ts):
    hl = RWKV_HALO
    d = D_MODEL
    g, sc, sh = g_ref[...], sc_ref[0], sh_ref[0]
    he_sc[pl.ds(hl, ts), :] = _modnorm(x_ref[0], g, sc, sh)

    @pl.when(is_first)
    def _():
        he_sc[pl.ds(0, hl), :] = jnp.zeros((hl, d), F32)

    @pl.when(jnp.logical_not(is_first))
    def _():
        he_sc[pl.ds(0, hl), :] = _modnorm(p_ref[0], g, sc, sh)

    @pl.when(is_last)
    def _():
        he_sc[pl.ds(hl + ts, hl), :] = jnp.zeros((hl, d), F32)

    @pl.when(jnp.logical_not(is_last))
    def _():
        he_sc[pl.ds(hl + ts, hl), :] = _modnorm(n_ref[0], g, sc, sh)

    h = he_sc[pl.ds(hl, ts), :]
    xx = 0.5 * (he_sc[pl.ds(hl - 1, ts), :] + he_sc[pl.ds(hl + 1, ts), :]) - h
    return h, xx


def _rwkv_proj_kernel(x_ref, p_ref, n_ref, g_ref, sc_ref, sh_ref, mu_ref, w_ref, o_ref,
                      he_sc, xx_sc, *, ts):
    i, m = pl.program_id(1), pl.program_id(2)

    @pl.when(m == 0)
    def _():
        _, xx = _rwkv_shifted(x_ref, p_ref, n_ref, g_ref, sc_ref, sh_ref, he_sc,
                              i == 0, i == pl.num_programs(1) - 1, ts)
        xx_sc[...] = xx

    xs = he_sc[pl.ds(RWKV_HALO, ts), :] + xx_sc[...] * mu_ref[0]
    o_ref[0] = jnp.dot(xs.astype(BF16), w_ref[0], preferred_element_type=F32).astype(o_ref.dtype)


def _halo_specs(ts, s, halo, d):
    rpb, nhb = ts // halo, s // halo
    prev = lambda bi, i, *_: (bi, jnp.maximum(i * rpb - 1, 0), 0)
    nxt = lambda bi, i, *_: (bi, jnp.minimum((i + 1) * rpb, nhb - 1), 0)
    return pl.BlockSpec((1, halo, d), prev), pl.BlockSpec((1, halo, d), nxt)


def rwkv_proj(x, g, scale, shift, mu, w_in, *, ts):
    b, s, d = x.shape
    prev, nxt = _halo_specs(ts, s, RWKV_HALO, d)
    return pl.pallas_call(
        functools.partial(_rwkv_proj_kernel, ts=ts),
        grid=(b, s // ts, 4),
        in_specs=[
            pl.BlockSpec((1, ts, d), lambda bi, i, m: (bi, i, 0)), prev, nxt,
            pl.BlockSpec((1, d), lambda bi, i, m: (0, 0)),
            pl.BlockSpec((1, 1, d), lambda bi, i, m: (bi, 0, 0)),
            pl.BlockSpec((1, 1, d), lambda bi, i, m: (bi, 0, 0)),
            pl.BlockSpec((1, 1, d), lambda bi, i, m: (m, 0, 0)),
            pl.BlockSpec((1, d, d), lambda bi, i, m: (m, 0, 0)),
        ],
        out_specs=pl.BlockSpec((1, ts, d), lambda bi, i, m: (bi, i, m)),
        out_shape=jax.ShapeDtypeStruct((b, s, 4 * d), BF16),
        scratch_shapes=[pltpu.VMEM((ts + 2 * RWKV_HALO, d), F32), pltpu.VMEM((ts, d), F32)],
        compiler_params=_params("parallel", "parallel", "arbitrary"),
        name="rwkv_proj",
    )(x, x, x, g, scale, shift, mu.reshape(6, 1, d), w_in)


def _neg_softplus_neg(x):
    return -(jnp.maximum(-x, 0.0) + jnp.log(1.0 + jnp.exp(-jnp.abs(x))))


def _rwkv_lora_kernel(x_ref, p_ref, n_ref, g_ref, sc_ref, sh_ref, mu_ref, w0_ref, w1_ref, w2_ref,
                      a0_ref, a1_ref, a2_ref, lw_ref, a_ref, he_sc, *, ts):
    i = pl.program_id(1)
    h, xx = _rwkv_shifted(x_ref, p_ref, n_ref, g_ref, sc_ref, sh_ref, he_sc,
                          i == 0, i == pl.num_programs(1) - 1, ts)
    xw = (h + xx * mu_ref[4]).astype(BF16)
    xa = (h + xx * mu_ref[5]).astype(BF16)
    for dr in range(2):
        t1 = jnp.tanh(jnp.dot(xw, w1_ref[dr], preferred_element_type=F32))
        wl = w0_ref[dr] + jnp.dot(t1.astype(BF16), w2_ref[dr], preferred_element_type=F32)
        lw_ref[dr, 0] = -jnp.exp(_neg_softplus_neg(wl) - 0.5)
        t2 = jnp.dot(xa, a1_ref[dr], preferred_element_type=F32)
        al = a0_ref[dr] + jnp.dot(t2.astype(BF16), a2_ref[dr], preferred_element_type=F32)
        a_ref[dr, 0] = jax.nn.sigmoid(al).astype(a_ref.dtype)


def rwkv_lora(x, g, scale, shift, mu, w0, w1, w2, a0, a1, a2, *, ts):
    b, s, d = x.shape
    lp = RWKV_LORA_PAD
    prev, nxt = _halo_specs(ts, s, RWKV_HALO, d)
    full = lambda *shape: pl.BlockSpec(shape, lambda bi, i: (0,) * len(shape))
    out = pl.BlockSpec((2, 1, ts, d), lambda bi, i: (0, bi, i, 0))
    return pl.pallas_call(
        functools.partial(_rwkv_lora_kernel, ts=ts),
        grid=(b, s // ts),
        in_specs=[
            pl.BlockSpec((1, ts, d), lambda bi, i: (bi, i, 0)), prev, nxt,
            full(1, d),
            pl.BlockSpec((1, 1, d), lambda bi, i: (bi, 0, 0)),
            pl.BlockSpec((1, 1, d), lambda bi, i: (bi, 0, 0)),
            full(6, 1, d), full(2, 1, d), full(2, d, lp), full(2, lp, d),
            full(2, 1, d), full(2, d, lp), full(2, lp, d),
        ],
        out_specs=[out, out],
        out_shape=[jax.ShapeDtypeStruct((2, b, s, d), F32), jax.ShapeDtypeStruct((2, b, s, d), BF16)],
        scratch_shapes=[pltpu.VMEM((ts + 2 * RWKV_HALO, d), F32)],
        compiler_params=_params("parallel", "parallel"),
        name="rwkv_lora",
    )(x, x, x, g, scale, shift, mu.reshape(6, 1, d), w0.reshape(2, 1, d), w1, w2,
      a0.reshape(2, 1, d), a1, a2)


def _hilo(x):
    hi = x.astype(BF16)
    return hi, (x - hi.astype(F32)).astype(BF16)


def _dot2(x, w):
    hi, lo = _hilo(x)
    return (jnp.dot(hi, w, preferred_element_type=F32) + jnp.dot(lo, w, preferred_element_type=F32))


def _nt(x, y):
    return lax.dot_general(x, y, (((1,), (1,)), ((), ())), preferred_element_type=F32)


def _bdot(x, y):
    return jnp.dot(x.astype(BF16), y.astype(BF16), preferred_element_type=F32)


def _rwkv_chunk_terms(lw, r, k, v, a, kk_scale, ka_scale, rk_scale, reverse):
    t = RWKV_CHUNK
    n = RWKV_N
    lane_t = lax.broadcasted_iota(jnp.int32, (t, LANES), 1)
    row_t = lax.broadcasted_iota(jnp.int32, (t, LANES), 0)
    m1 = lane_t < n
    col_t = lane_t % n
    r128 = lax.broadcasted_iota(jnp.int32, (LANES, LANES), 0)
    c128 = lax.broadcasted_iota(jnp.int32, (LANES, LANES), 1)
    same_head = (r128 // n) == (c128 // n)
    ones_bd = jnp.where(same_head, 1.0, 0.0).astype(BF16)
    rt = lax.broadcasted_iota(jnp.int32, (t, t), 0)
    ct = lax.broadcasted_iota(jnp.int32, (t, t), 1)
    if reverse:
        strict2, incl2, tri = col_t > row_t, col_t >= row_t, ct >= rt
    else:
        strict2, incl2, tri = col_t < row_t, col_t <= row_t, ct <= rt
    tri = jnp.where(tri, 1.0, 0.0).astype(BF16)

    def sel(top, bot):
        return jnp.where(m1, top, bot)

    def stack(x):
        return jnp.concatenate([jnp.where(m1, x, 0.0), jnp.where(m1, 0.0, x)], axis=0)

    kk = k * kk_scale
    nrm = jnp.sqrt(_dot2(kk * kk, ones_bd))
    kk = kk / jnp.maximum(nrm, 1e-12)
    kd = k * (1.0 + (a - 1.0) * ka_scale)
    bonus = _dot2(r * kd * rk_scale, ones_bd) * v

    c = _dot2_left(tri, lw)
    e_pos, e_neg = jnp.exp(c), jnp.exp(-c)
    at = -kk * jnp.exp(c - lw)
    rt_ = r * e_pos
    bt = kk * a * e_neg
    kt = kd * e_neg
    big = _nt(jnp.concatenate([at, rt_], axis=0).astype(BF16),
              jnp.concatenate([stack(bt), stack(kt)], axis=0).astype(BF16))
    aab = jnp.where(strict2, big[:t, :LANES], 0.0)
    aak = jnp.where(strict2, big[:t, LANES:], 0.0)
    rb = jnp.where(incl2, big[t:, :LANES], 0.0)
    rk = jnp.where(incl2, big[t:, LANES:], 0.0)

    abd = stack(aab)
    dinv = jnp.where(r128 == c128, 1.0, 0.0)
    sz = 1
    while sz < t:
        hi_r, lo_c = (r128 % (2 * sz)) >= sz, (c128 % (2 * sz)) < sz
        if reverse:
            off = jnp.logical_and(jnp.logical_not(hi_r), jnp.logical_not(lo_c))
        else:
            off = jnp.logical_and(hi_r, lo_c)
        off = jnp.logical_and(off, (r128 // (2 * sz)) == (c128 // (2 * sz)))
        dinv = dinv + _bdot(_bdot(dinv, jnp.where(off, abd, 0.0)), dinv)
        sz *= 2

    vst = stack(v)
    aakv = _bdot(aak, vst)
    w = jnp.concatenate([at, aakv], axis=1)
    x = _bdot(dinv, jnp.concatenate([w, w], axis=0))
    atp = sel(x[:t, :LANES], x[t:, :LANES])
    u0 = sel(x[:t, LANES:], x[t:, LANES:])
    y = _bdot(rb, jnp.concatenate([stack(atp), stack(u0)], axis=1))
    rp = rt_ + y[:, :LANES]
    o0 = y[:, LANES:] + _bdot(rk, vst)
    gmt = jnp.where(same_head, _bdot(atp.T, bt), 0.0)
    zty = jnp.where(same_head, _bdot(jnp.concatenate([u0, v], axis=0).T,
                                     jnp.concatenate([bt, kt], axis=0)), 0.0)
    decay_row = e_pos[0:1, :] if reverse else e_pos[t - 1:t, :]
    return rp, o0, gmt, zty, decay_row, bonus


def _dot2_left(w, x):
    hi, lo = _hilo(x)
    return (jnp.dot(w, hi, preferred_element_type=F32) + jnp.dot(w, lo, preferred_element_type=F32))


def _rwkv_scan_kernel(lw_ref, a_ref, r_ref, k_ref, v_ref, kk_ref, ka_ref, rk_ref, o_ref, bo_ref,
                      s_sc, *, ts, reverse):
    @pl.when(pl.program_id(2) == 0)
    def _():
        s_sc[...] = jnp.zeros(s_sc.shape, F32)

    t = RWKV_CHUNK
    nch = ts // t
    order = range(nch - 1, -1, -1) if reverse else range(nch)
    terms = {}
    for ci in order:
        rows = slice(ci * t, (ci + 1) * t)
        terms[ci] = _rwkv_chunk_terms(
            lw_ref[0, 0, rows, :], r_ref[0, rows, :].astype(F32), k_ref[0, rows, :].astype(F32),
            v_ref[0, rows, :].astype(F32), a_ref[0, 0, rows, :].astype(F32),
            kk_ref[...], ka_ref[...], rk_ref[...], reverse)
    s = s_sc[...]
    for ci in order:
        rp, o0, gmt, zty, decay_row, bonus = terms[ci]
        s_hi, s_lo = _hilo(s)
        rp_b = rp.astype(BF16)
        o_ref[0, ci * t:(ci + 1) * t, :] = o0 + _nt(rp_b, s_hi) + _nt(rp_b, s_lo)
        bo_ref[0, ci * t:(ci + 1) * t, :] = bonus.astype(bo_ref.dtype)
        gmt_b = gmt.astype(BF16)
        s = (s + jnp.dot(s_hi, gmt_b, preferred_element_type=F32)
             + jnp.dot(s_lo, gmt_b, preferred_element_type=F32) + zty) * decay_row
    s_sc[...] = s


def rwkv_scan(lw, a, rkvg, k_k, k_a, r_k, dr, *, ts):
    _, b, s, d = lw.shape
    nt = s // ts
    npair = d // LANES
    reverse = dr == 1
    tmap = (lambda i: nt - 1 - i) if reverse else (lambda i: i)
    dspec = pl.BlockSpec((1, 1, ts, LANES), lambda bi, hp, i: (dr, bi, tmap(i), hp))
    col = lambda m: pl.BlockSpec((1, ts, LANES), lambda bi, hp, i: (bi, tmap(i), m * npair + hp))
    par = pl.BlockSpec((1, LANES), lambda bi, hp, i: (0, hp))
    out = pl.BlockSpec((1, ts, LANES), lambda bi, hp, i: (bi, tmap(i), hp))
    return pl.pallas_call(
        functools.partial(_rwkv_scan_kernel, ts=ts, reverse=reverse),
        grid=(b, npair, nt),
        in_specs=[dspec, dspec, col(0), col(1), col(2), par, par, par],
        out_specs=[out, out],
        out_shape=[jax.ShapeDtypeStruct((b, s, d), F32), jax.ShapeDtypeStruct((b, s, d), BF16)],
        scratch_shapes=[pltpu.VMEM((LANES, LANES), F32)],
        compiler_params=_params("parallel", "parallel", "arbitrary"),
        name="rwkv_scan_bwd" if reverse else "rwkv_scan_fwd",
    )(lw, a, rkvg, rkvg, rkvg, k_k, k_a, r_k)


def _rwkv_out_kernel(of_ref, ob_ref, bf_ref, bb_ref, g_ref, lg_ref, lb_ref, w_ref, x_ref, gm_ref,
                     gp_ref, o_ref, a_sc):
    r128 = lax.broadcasted_iota(jnp.int32, (LANES, LANES), 0)
    c128 = lax.broadcasted_iota(jnp.int32, (LANES, LANES), 1)
    avg = jnp.where((r128 // RWKV_N) == (c128 // RWKV_N), 1.0 / RWKV_N, 0.0).astype(BF16)
    for c in range(D_MODEL // LANES):
        cols = slice(c * LANES, (c + 1) * LANES)
        o = of_ref[0, :, cols] + ob_ref[0, :, cols]
        cen = o - _dot2(o, avg)
        var = _dot2(cen * cen, avg)
        on = cen * lax.rsqrt(var + RWKV_GN_EPS)
        y = (on * lg_ref[:, cols] + lb_ref[:, cols]
             + bf_ref[0, :, cols].astype(F32) + bb_ref[0, :, cols].astype(F32))
        a_sc[:, cols] = (y * _silu(g_ref[0, :, cols].astype(F32))).astype(BF16)
    _finish(a_sc[...], w_ref, x_ref, gm_ref, gp_ref, o_ref)


def rwkv_out(o_f, o_b, bo_f, bo_b, rkvg, ln_g, ln_b, w, x, gate_mod, g_post, *, ts):
    b, s, d = x.shape
    row = lambda bi, i: (bi, i, 0)
    vec = pl.BlockSpec((1, d), lambda bi, i: (0, 0))
    return pl.pallas_call(
        _rwkv_out_kernel,
        grid=(b, s // ts),
        in_specs=[pl.BlockSpec((1, ts, d), row)] * 4 + [
            pl.BlockSpec((1, ts, d), lambda bi, i: (bi, i, 3)), vec, vec,
            pl.BlockSpec((d, d), lambda bi, i: (0, 0)),
            pl.BlockSpec((1, ts, d), row),
            pl.BlockSpec((1, 1, d), lambda bi, i: (bi, 0, 0)),
            vec,
        ],
        out_specs=pl.BlockSpec((1, ts, d), row),
        out_shape=jax.ShapeDtypeStruct((b, s, d), F32),
        scratch_shapes=[pltpu.VMEM((ts, d), BF16)],
        compiler_params=_params("parallel", "parallel"),
        name="rwkv_out",
    )(o_f, o_b, bo_f, bo_b, rkvg, ln_g, ln_b, w, x, gate_mod, g_post)


def _prep_weights(p):
    d = D_MODEL
    w = {}
    wi = p['mla_w_in'][0]
    q_lat, kv_lat, k_pe, gate = jnp.split(
        wi, [MLA_Q_RANK, MLA_Q_RANK + MLA_KV_RANK, MLA_Q_RANK + MLA_KV_RANK + MLA_ROPE], axis=1)
    w['mla_in'] = jnp.concatenate(
        [gate, q_lat, kv_lat, k_pe, jnp.zeros((d, LANES - MLA_ROPE), F32)], axis=1).astype(BF16)
    wq = p['mla_w_q_up'][0].reshape(MLA_Q_RANK, MLA_HEADS, MLA_NOPE + MLA_ROPE)
    wq = jnp.pad(wq, ((0, 0), (0, 0), (0, MLA_QK_PAD - MLA_NOPE - MLA_ROPE)))
    w['mla_q'] = wq.reshape(MLA_Q_RANK, MLA_HEADS * MLA_QK_PAD).astype(BF16)
    w['mla_kv'] = p['mla_w_kv_up'][0].astype(BF16)
    w['mla_out'] = p['mla_w_out'][0].astype(BF16)
    w['diff_in'] = p['diff_w_in'][0].astype(BF16)
    w['diff_out'] = p['diff_w_out'][0].astype(BF16)
    w['lru_in'] = p['lru_w_in'][0].astype(BF16)
    w['lru_out'] = p['lru_w_out'][0].astype(BF16)
    gw = p['lru_gate_w'][0]
    w['lru_gate'] = jnp.concatenate([gw[:, 0], gw[:, 1]], axis=-1).astype(BF16)
    w['rwkv_in'] = p['rwkv_w_in'][0].astype(BF16)
    pad = RWKV_LORA_PAD - RWKV_LORA
    for nm in ('w1', 'a1'):
        w['rwkv_' + nm] = jnp.pad(p['rwkv_' + nm][0], ((0, 0), (0, 0), (0, pad))).astype(BF16)
    for nm in ('w2', 'a2'):
        w['rwkv_' + nm] = jnp.pad(p['rwkv_' + nm][0], ((0, 0), (0, pad), (0, 0))).astype(BF16)
    w['rwkv_out'] = p['rwkv_w_out'][0].astype(BF16)
    return w


def _mla_layer(x, pre_g, scale, shift, gate_mod, post_g, p, w, tabs):
    mla_tabs = tabs['mla']
    z = modmm(x, pre_g, scale, shift, w['mla_in'], ts=512, tn=640,
              rope=(24, 25, MLA_ROPE // 2, 1, 0), tabs=mla_tabs)
    q = normmm(z, 4, p['mla_q_norm_g'], w['mla_q'], ts=1024, tn=512,
               rope=(0, 2 * MLA_HEADS, MLA_ROPE // 2, 2, 1), tabs=mla_tabs)
    kv = normmm(z, 5, p['mla_kv_norm_g'], w['mla_kv'], ts=1024, tn=512)
    o = mla_attention(q, kv, z, 24, tq=512, tk=512)
    return outmm([o], z, 0, w['mla_out'], x, gate_mod, post_g, ts=256)


def _diff_layer(x, layer_idx, pre_g, scale, shift, gate_mod, post_g, p, w, tabs):
    z = modmm(x, pre_g, scale, shift, w['diff_in'], ts=512, tn=512,
              rope=(0, 2 * DIFF_HEADS, DIFF_ROT // 2, 1, 0), tabs=tabs['diff'])
    lam_init = 0.8 - 0.6 * math.exp(-0.3 * layer_idx)
    o = diff_attention(z, p['diff_lambda'][0], p['diff_subln_g'][0], lam_init, tq=256, tk=512)
    return outmm([o], z, 3, w['diff_out'], x, gate_mod, post_g, ts=256)


def _lru_layer(x, pre_g, scale, shift, gate_mod, post_g, p, w):
    z = modmm(x, pre_g, scale, shift, w['lru_in'], ts=512, tn=512)
    y_f, y_b = lru_scan(z, p['lru_conv_w'][0], p['lru_conv_b'], w['lru_gate'],
                        p['lru_gate_b'][0].reshape(4, LRU_WIDTH), p['lru_lambda'][0], ts=256)
    return outmm([y_f, y_b], z, 1, w['lru_out'], x, gate_mod, post_g, ts=256)


def _rwkv_layer(x, pre_g, scale, shift, gate_mod, post_g, p, w):
    mu = p['rwkv_mu'][0]
    rkvg = rwkv_proj(x, pre_g, scale, shift, mu, w['rwkv_in'], ts=512)
    lw, a = rwkv_lora(x, pre_g, scale, shift, mu, p['rwkv_w0'][0], w['rwkv_w1'], w['rwkv_w2'],
                      p['rwkv_a0'][0], w['rwkv_a1'], w['rwkv_a2'], ts=256)
    k_k = p['rwkv_k_k']
    k_a = p['rwkv_k_a']
    r_k = p['rwkv_r_k'].reshape(1, D_MODEL)
    o_f, bo_f = rwkv_scan(lw, a, rkvg, k_k, k_a, r_k, 0, ts=128)
    o_b, bo_b = rwkv_scan(lw, a, rkvg, k_k, k_a, r_k, 1, ts=128)
    return rwkv_out(o_f, o_b, bo_f, bo_b, rkvg, p['rwkv_ln_g'], p['rwkv_ln_b'], w['rwkv_out'],
                    x, gate_mod, post_g, ts=256)


def _trunk(x, mods, p, w):
    b, s, d = x.shape
    tabs = {
        'mla': _rope_tables(s, 1, LANES, MLA_ROPE, MLA_THETA, 0.0),
        'diff': _rope_tables(s, LANES // DIFF_HD, DIFF_HD, DIFF_ROT, ROPE_THETA, 1.0),
    }
    for i in range(DEPTH):
        shift, scale, gate_mod = (mods[i, :, k * d:(k + 1) * d].reshape(b, 1, d) for k in range(3))
        pre_g = p['norm_pre_g'][i:i + 1]
        post_g = p['norm_post_g'][i:i + 1]
        args = (pre_g, scale, shift, gate_mod, post_g, p, w)
        if i % 4 == 0:
            x = _mla_layer(x, *args, tabs)
        elif i % 4 == 1:
            x = _diff_layer(x, i, *args, tabs)
        elif i % 4 == 2:
            x = _lru_layer(x, *args)
        else:
            x = _rwkv_layer(x, *args)
    return x


def kernel(x_prompt, x_sample, c_prompt, c_sample, ada_w, ada_b, norm_pre_g, norm_post_g, mla_w_in, mla_q_norm_g, mla_kv_norm_g, mla_w_q_up, mla_w_kv_up, mla_w_out, diff_w_in, diff_lambda, diff_subln_g, diff_w_out, lru_w_in, lru_conv_w, lru_conv_b, lru_gate_w, lru_gate_b, lru_lambda, lru_w_out, rwkv_mu, rwkv_w_in, rwkv_w0, rwkv_w1, rwkv_w2, rwkv_a0, rwkv_a1, rwkv_a2, rwkv_k_k, rwkv_k_a, rwkv_r_k, rwkv_ln_g, rwkv_ln_b, rwkv_w_out):
    p = dict(
        norm_pre_g=norm_pre_g, norm_post_g=norm_post_g,
        mla_w_in=mla_w_in, mla_q_norm_g=mla_q_norm_g, mla_kv_norm_g=mla_kv_norm_g,
        mla_w_q_up=mla_w_q_up, mla_w_kv_up=mla_w_kv_up, mla_w_out=mla_w_out,
        diff_w_in=diff_w_in, diff_lambda=diff_lambda, diff_subln_g=diff_subln_g,
        diff_w_out=diff_w_out,
        lru_w_in=lru_w_in, lru_conv_w=lru_conv_w, lru_conv_b=lru_conv_b, lru_gate_w=lru_gate_w,
        lru_gate_b=lru_gate_b, lru_lambda=lru_lambda, lru_w_out=lru_w_out,
        rwkv_mu=rwkv_mu, rwkv_w_in=rwkv_w_in, rwkv_w0=rwkv_w0, rwkv_w1=rwkv_w1, rwkv_w2=rwkv_w2,
        rwkv_a0=rwkv_a0, rwkv_a1=rwkv_a1, rwkv_a2=rwkv_a2, rwkv_k_k=rwkv_k_k, rwkv_k_a=rwkv_k_a,
        rwkv_r_k=rwkv_r_k, rwkv_ln_g=rwkv_ln_g, rwkv_ln_b=rwkv_ln_b, rwkv_w_out=rwkv_w_out,
    )
    w = _prep_weights(p)
    nb = x_prompt.shape[0]
    mods = ada_mod(jnp.concatenate([c_prompt, c_sample], axis=0), ada_w, ada_b)
    y_prompt = _trunk(x_prompt, mods[:, :nb], p, w)
    y_sample = _trunk(x_sample, mods[:, nb:], p, w)
    return (y_prompt, y_sample)
```

```python
import functools
import math

import jax
import jax.numpy as jnp
from jax import lax
from jax.experimental import pallas as pl
from jax.experimental.pallas import tpu as pltpu

F32 = jnp.float32
BF16 = jnp.bfloat16

D_MODEL = 2048
DEPTH = 4
NORM_EPS = 1e-6
LANES = 128
VMEM_LIMIT = 56 * 1024 * 1024

MLA_HEADS = 16
MLA_Q_RANK = 512
MLA_KV_RANK = 512
MLA_NOPE = 128
MLA_ROPE = 64
MLA_V = 128
MLA_THETA = 10000.0
MLA_QK_PAD = 256

DIFF_HEADS = 16
DIFF_HD = 64
DIFF_ROT = 16
ROPE_THETA = 500000.0
DIFF_SUBLN_EPS = 1e-5

LRU_WIDTH = 2048
LRU_BLOCKS = 16
LRU_BW = 128
LRU_C = 8.0
LRU_HALO = 16

RWKV_N = 64
RWKV_HEADS = 32
RWKV_LORA = 96
RWKV_LORA_PAD = 128
RWKV_GN_EPS = 64e-5
RWKV_CHUNK = 64


def _params(*sem):
    return pltpu.CompilerParams(dimension_semantics=sem, vmem_limit_bytes=VMEM_LIMIT)


def _silu(x):
    return x * jax.nn.sigmoid(x)


def _rope128(y, c, s1, s2, shift):
    return y * c + pltpu.roll(y, LANES - shift, 1) * s1 + pltpu.roll(y, shift, 1) * s2


def _rope_tables(seq, heads_per_chunk, head_dim, rot_dim, theta, pad_cos):
    half = rot_dim // 2
    inv = 1.0 / (theta ** (jnp.arange(0, rot_dim, 2, dtype=F32) / rot_dim))
    ang = jnp.arange(seq, dtype=F32)[:, None] * inv[None, :]
    cos, sin = jnp.cos(ang), jnp.sin(ang)
    rest = head_dim - rot_dim
    c = jnp.concatenate([cos, cos, jnp.full((seq, rest), pad_cos, F32)], axis=-1)
    s1 = jnp.concatenate([-sin, jnp.zeros((seq, half + rest), F32)], axis=-1)
    s2 = jnp.concatenate([jnp.zeros((seq, half), F32), sin, jnp.zeros((seq, rest), F32)], axis=-1)
    tile = lambda t: jnp.tile(t, (1, heads_per_chunk))
    return tile(c), tile(s1), tile(s2)


def _ada_kernel(c_ref, w_ref, b_ref, o_ref):
    cs = _silu(c_ref[...]).astype(BF16)
    o_ref[0] = jnp.dot(cs, w_ref[0].astype(BF16), preferred_element_type=F32) + b_ref[0]


def ada_mod(c, ada_w, ada_b):
    nb, d = c.shape
    depth, _, n = ada_w.shape
    tn = 512
    return pl.pallas_call(
        _ada_kernel,
        grid=(depth, n // tn),
        in_specs=[
            pl.BlockSpec((nb, d), lambda l, j: (0, 0)),
            pl.BlockSpec((1, d, tn), lambda l, j: (l, 0, j)),
            pl.BlockSpec((1, 1, tn), lambda l, j: (l, 0, j)),
        ],
        out_specs=pl.BlockSpec((1, nb, tn), lambda l, j: (l, 0, j)),
        out_shape=jax.ShapeDtypeStruct((depth, nb, n), F32),
        compiler_params=_params("parallel", "parallel"),
        name="ada_mod",
    )(c, ada_w, ada_b.reshape(depth, 1, n))


def _modnorm(x, g, sc, sh):
    ms = jnp.mean(x * x, axis=-1, keepdims=True)
    return x * lax.rsqrt(ms + NORM_EPS) * g * (1.0 + sc) + sh


def _store_with_rope(y, o_ref, j, tn, rope, tabs):
    if rope is None:
        o_ref[0] = y.astype(o_ref.dtype)
        return
    lo, hi, shift, stride, phase = rope
    c_ref, s1_ref, s2_ref = tabs
    cpt = tn // LANES
    for c in range(cpt):
        gc = j * cpt + c
        yc = y[:, c * LANES:(c + 1) * LANES]
        is_rope = jnp.logical_and(jnp.logical_and(gc >= lo, gc < hi), gc % stride == phase)

        @pl.when(is_rope)
        def _(yc=yc, c=c):
            o_ref[0, :, c * LANES:(c + 1) * LANES] = _rope128(
                yc, c_ref[...], s1_ref[...], s2_ref[...], shift).astype(o_ref.dtype)

        @pl.when(jnp.logical_not(is_rope))
        def _(yc=yc, c=c):
            o_ref[0, :, c * LANES:(c + 1) * LANES] = yc.astype(o_ref.dtype)


def _modmm_kernel(x_ref, g_ref, sc_ref, sh_ref, w_ref, *rest, tn, rope):
    tabs, (o_ref, h_ref) = rest[:-2], rest[-2:]
    j = pl.program_id(2)

    @pl.when(j == 0)
    def _():
        h_ref[...] = _modnorm(x_ref[0], g_ref[...], sc_ref[0], sh_ref[0]).astype(BF16)

    y = jnp.dot(h_ref[...], w_ref[...], preferred_element_type=F32)
    _store_with_rope(y, o_ref, j, tn, rope, tabs)


def modmm(x, g, scale, shift, w, *, ts, tn, out_dtype=BF16, rope=None, tabs=()):
    b, s, d = x.shape
    n = w.shape[1]
    tab_specs = [pl.BlockSpec((ts, LANES), lambda bi, i, j: (i, 0)) for _ in tabs]
    return pl.pallas_call(
        functools.partial(_modmm_kernel, tn=tn, rope=rope),
        grid=(b, s // ts, n // tn),
        in_specs=[
            pl.BlockSpec((1, ts, d), lambda bi, i, j: (bi, i, 0)),
            pl.BlockSpec((1, d), lambda bi, i, j: (0, 0)),
            pl.BlockSpec((1, 1, d), lambda bi, i, j: (bi, 0, 0)),
            pl.BlockSpec((1, 1, d), lambda bi, i, j: (bi, 0, 0)),
            pl.BlockSpec((d, tn), lambda bi, i, j: (0, j)),
        ] + tab_specs,
        out_specs=pl.BlockSpec((1, ts, tn), lambda bi, i, j: (bi, i, j)),
        out_shape=jax.ShapeDtypeStruct((b, s, n), out_dtype),
        scratch_shapes=[pltpu.VMEM((ts, d), BF16)],
        compiler_params=_params("parallel", "parallel", "arbitrary"),
        name="modmm",
    )(x, g, scale, shift, w, *tabs)


def _normmm_kernel(x_ref, g_ref, w_ref, *rest, tn, rope):
    tabs, (o_ref, h_ref) = rest[:-2], rest[-2:]
    j = pl.program_id(2)

    @pl.when(j == 0)
    def _():
        x = x_ref[0].astype(F32)
        ms = jnp.mean(x * x, axis=-1, keepdims=True)
        h_ref[...] = (x * lax.rsqrt(ms + NORM_EPS) * g_ref[...]).astype(BF16)

    y = jnp.dot(h_ref[...], w_ref[...], preferred_element_type=F32)
    _store_with_rope(y, o_ref, j, tn, rope, tabs)


def normmm(z, xcol, g, w, *, ts, tn, rope=None, tabs=()):
    b, s, _ = z.shape
    k, n = w.shape
    tab_specs = [pl.BlockSpec((ts, LANES), lambda bi, i, j: (i, 0)) for _ in tabs]
    return pl.pallas_call(
        functools.partial(_normmm_kernel, tn=tn, rope=rope),
        grid=(b, s // ts, n // tn),
        in_specs=[
            pl.BlockSpec((1, ts, k), lambda bi, i, j: (bi, i, xcol)),
            pl.BlockSpec((1, k), lambda bi, i, j: (0, 0)),
            pl.BlockSpec((k, tn), lambda bi, i, j: (0, j)),
        ] + tab_specs,
        out_specs=pl.BlockSpec((1, ts, tn), lambda bi, i, j: (bi, i, j)),
        out_shape=jax.ShapeDtypeStruct((b, s, n), BF16),
        scratch_shapes=[pltpu.VMEM((ts, k), BF16)],
        compiler_params=_params("parallel", "parallel", "arbitrary"),
        name="normmm",
    )(z, g, w, *tabs)


LOG2E = math.log2(math.e)


def _flash(q, k_ref, v_ref, m_sc, l_sc, acc_sc, tk):
    seq = k_ref.shape[0]
    m_sc[...] = jnp.full(m_sc.shape, -jnp.inf, F32)
    l_sc[...] = jnp.zeros(l_sc.shape, F32)
    acc_sc[...] = jnp.zeros(acc_sc.shape, F32)
    nck = tk // LANES

    def body(c, carry):
        off = pl.multiple_of(c * tk, tk)
        k = k_ref[pl.ds(off, tk), :]
        s = lax.dot_general(q, k, (((1,), (1,)), ((), ())), preferred_element_type=F32)
        sc = [s[:, i * LANES:(i + 1) * LANES] for i in range(nck)]
        m_part = functools.reduce(jnp.maximum, sc)
        m_prev = m_sc[...]
        m_new = jnp.maximum(m_prev, jnp.max(m_part, axis=-1, keepdims=True))
        alpha = jnp.exp2(m_prev - m_new)
        pc = [jnp.exp2(x - m_new) for x in sc]
        l_sc[...] = alpha * l_sc[...] + functools.reduce(jnp.add, pc)
        p = jnp.concatenate([x.astype(BF16) for x in pc], axis=1)
        acc_sc[...] = alpha * acc_sc[...] + jnp.dot(
            p, v_ref[pl.ds(off, tk), :], preferred_element_type=F32)
        m_sc[...] = m_new
        return carry

    lax.fori_loop(0, seq // tk, body, 0, unroll=2)


def _row_total(l):
    return jnp.sum(l, axis=-1, keepdims=True)


def _mla_attn_kernel(q_ref, kn_ref, v_ref, kpe_ref, o_ref, k_sc, m_sc, l_sc, acc_sc, *, tk, scale):
    @pl.when(pl.program_id(2) == 0)
    def _():
        k_sc[:, :LANES] = kn_ref[0]
        k_sc[:, LANES:] = kpe_ref[0]

    q = (q_ref[0].astype(F32) * (scale * LOG2E)).astype(BF16)
    _flash(q, k_sc, v_ref.at[0], m_sc, l_sc, acc_sc, tk)
    o_ref[0] = (acc_sc[...] / _row_total(l_sc[...])).astype(o_ref.dtype)


def mla_attention(q, kv, z, kpe_col, *, tq, tk):
    b, s, _ = q.shape
    scale = (MLA_NOPE + MLA_ROPE) ** -0.5
    return pl.pallas_call(
        functools.partial(_mla_attn_kernel, tk=tk, scale=scale),
        grid=(b, MLA_HEADS, s // tq),
        in_specs=[
            pl.BlockSpec((1, tq, MLA_QK_PAD), lambda bi, h, i: (bi, i, h)),
            pl.BlockSpec((1, s, LANES), lambda bi, h, i: (bi, 0, 2 * h)),
            pl.BlockSpec((1, s, LANES), lambda bi, h, i: (bi, 0, 2 * h + 1)),
            pl.BlockSpec((1, s, LANES), lambda bi, h, i: (bi, 0, kpe_col)),
        ],
        out_specs=pl.BlockSpec((1, tq, MLA_V), lambda bi, h, i: (bi, i, h)),
        out_shape=jax.ShapeDtypeStruct((b, s, MLA_HEADS * MLA_V), BF16),
        scratch_shapes=[
            pltpu.VMEM((s, MLA_QK_PAD), BF16),
            pltpu.VMEM((tq, LANES), F32),
            pltpu.VMEM((tq, LANES), F32),
            pltpu.VMEM((tq, MLA_V), F32),
        ],
        compiler_params=_params("parallel", "parallel", "arbitrary"),
        name="mla_attention",
    )(q, kv, kv, z)


def _diff_attn_kernel(q_ref, k_ref, v_ref, lam_ref, g_ref, o_ref, m_sc, l_sc, acc_sc,
                      *, tq, tk, scale, lam_init):
    q = q_ref[0].astype(F32) * (scale * LOG2E)
    lane = lax.broadcasted_iota(jnp.int32, q.shape, 1)
    q1 = jnp.where(lane < DIFF_HD, q, 0.0).astype(BF16)
    q2 = jnp.where(lane >= DIFF_HD, q, 0.0).astype(BF16)
    _flash(jnp.concatenate([q1, q2], axis=0), k_ref.at[0], v_ref.at[0], m_sc, l_sc, acc_sc, tk)
    lam = lam_ref[...]
    lam_full = (jnp.exp(jnp.sum(lam[0:1] * lam[1:2], axis=-1, keepdims=True))
                - jnp.exp(jnp.sum(lam[2:3] * lam[3:4], axis=-1, keepdims=True)) + lam_init)
    o1 = acc_sc[:tq, :] / _row_total(l_sc[:tq, :])
    o2 = acc_sc[tq:, :] / _row_total(l_sc[tq:, :])
    o = o1 - lam_full * o2
    ms = jnp.mean(o * o, axis=-1, keepdims=True)
    o = o * lax.rsqrt(ms + DIFF_SUBLN_EPS) * g_ref[...] * (1.0 - lam_init)
    o_ref[0] = o.astype(o_ref.dtype)


def diff_attention(z, lam, subln_g, lam_init, *, tq, tk):
    b, s, _ = z.shape
    scale = DIFF_HD ** -0.5
    nh = DIFF_HEADS
    return pl.pallas_call(
        functools.partial(_diff_attn_kernel, tq=tq, tk=tk, scale=scale, lam_init=lam_init),
        grid=(b, nh, s // tq),
        in_specs=[
            pl.BlockSpec((1, tq, LANES), lambda bi, h, i: (bi, i, h)),
            pl.BlockSpec((1, s, LANES), lambda bi, h, i: (bi, 0, nh + h)),
            pl.BlockSpec((1, s, LANES), lambda bi, h, i: (bi, 0, 2 * nh + h)),
            pl.BlockSpec((4, DIFF_HD), lambda bi, h, i: (0, 0)),
            pl.BlockSpec((1, LANES), lambda bi, h, i: (0, 0)),
        ],
        out_specs=pl.BlockSpec((1, tq, LANES), lambda bi, h, i: (bi, i, h)),
        out_shape=jax.ShapeDtypeStruct((b, s, nh * LANES), BF16),
        scratch_shapes=[
            pltpu.VMEM((2 * tq, LANES), F32),
            pltpu.VMEM((2 * tq, LANES), F32),
            pltpu.VMEM((2 * tq, LANES), F32),
        ],
        compiler_params=_params("parallel", "parallel", "arbitrary"),
        name="diff_attention",
    )(z, z, z, lam, subln_g.reshape(1, LANES))


def _finish(a, w_ref, x_ref, gm_ref, gp_ref, o_ref):
    y = jnp.dot(a.astype(BF16), w_ref[...], preferred_element_type=F32)
    ms = jnp.mean(y * y, axis=-1, keepdims=True)
    o_ref[0] = x_ref[0] + gm_ref[0] * (y * lax.rsqrt(ms + NORM_EPS) * gp_ref[...])


def _outmm_kernel(a_ref, g_ref, w_ref, x_ref, gm_ref, gp_ref, o_ref):
    a = a_ref[0].astype(F32) * _silu(g_ref[0].astype(F32))
    _finish(a, w_ref, x_ref, gm_ref, gp_ref, o_ref)


def _outmm2_kernel(a1_ref, a2_ref, g_ref, w_ref, x_ref, gm_ref, gp_ref, o_ref):
    a = (a1_ref[0] + a2_ref[0]) * _silu(g_ref[0].astype(F32))
    _finish(a, w_ref, x_ref, gm_ref, gp_ref, o_ref)


def outmm(acts, zgate, gcol, w, x, gate_mod, g_post, *, ts):
    b, s, d = x.shape
    row = lambda bi, i: (bi, i, 0)
    kern = _outmm_kernel if len(acts) == 1 else _outmm2_kernel
    return pl.pallas_call(
        kern,
        grid=(b, s // ts),
        in_specs=[pl.BlockSpec((1, ts, d), row) for _ in acts] + [
            pl.BlockSpec((1, ts, d), lambda bi, i: (bi, i, gcol)),
            pl.BlockSpec((d, d), lambda bi, i: (0, 0)),
            pl.BlockSpec((1, ts, d), row),
            pl.BlockSpec((1, 1, d), lambda bi, i: (bi, 0, 0)),
            pl.BlockSpec((1, d), lambda bi, i: (0, 0)),
        ],
        out_specs=pl.BlockSpec((1, ts, d), row),
        out_shape=jax.ShapeDtypeStruct((b, s, d), F32),
        compiler_params=_params("parallel", "parallel"),
        name="outmm",
    )(*acts, zgate, w, x, gate_mod, g_post)


def _lru_prepare(xe_sc, x_ref, prev_ref, next_ref, is_first, is_last, cw_ref, cb_ref, gw_ref,
                 gb_ref, sp_ref, a_sc, u_sc, d, ts, edge_row):
    w = LRU_WIDTH
    h = LRU_HALO
    xe_sc[pl.ds(h, ts), :] = x_ref[0].astype(F32)

    @pl.when(is_first)
    def _():
        xe_sc[pl.ds(0, h), :] = jnp.zeros((h, w), F32)

    @pl.when(jnp.logical_not(is_first))
    def _():
        xe_sc[pl.ds(0, h), :] = prev_ref[0].astype(F32)

    @pl.when(is_last)
    def _():
        xe_sc[pl.ds(h + ts, h), :] = jnp.zeros((h, w), F32)

    @pl.when(jnp.logical_not(is_last))
    def _():
        xe_sc[pl.ds(h + ts, h), :] = next_ref[0].astype(F32)

    xc = cb_ref[...] + sum(cw_ref[i:i + 1, :] * xe_sc[pl.ds(h - 1 + i, ts), :] for i in range(4))
    row = lax.broadcasted_iota(jnp.int32, (ts, LRU_BW), 0)
    for n in range(LRU_BLOCKS):
        cols = slice(n * LRU_BW, (n + 1) * LRU_BW)
        xn = xc[:, cols]
        g = jnp.dot(xn.astype(BF16), gw_ref[d, n], preferred_element_type=F32)
        r_t = jax.nn.sigmoid(g[:, :LRU_BW] + gb_ref[2 * d:2 * d + 1, cols])
        i_t = jax.nn.sigmoid(g[:, LRU_BW:] + gb_ref[2 * d + 1:2 * d + 2, cols])
        log_a = -LRU_C * r_t * sp_ref[d:d + 1, cols]
        a_t = jnp.exp(log_a)
        mult = jnp.sqrt(-jnp.tanh(log_a) * (1.0 + a_t * a_t))
        mult = jnp.where(row == edge_row, 1.0, mult)
        a_sc[:, cols] = a_t
        u_sc[:, cols] = mult * i_t * xn


def _lru_kernel(xf_ref, pf_ref, nf_ref, xb_ref, pb_ref, nb_ref, cw_ref, cb_ref, gw_ref, gb_ref,
                lam_ref, yf_ref, yb_ref, xe_sc, af_sc, uf_sc, ab_sc, ub_sc, hf_sc, hb_sc, sp_sc,
                *, ts):
    i = pl.program_id(1)
    nt = pl.num_programs(1)

    @pl.when(i == 0)
    def _():
        hf_sc[...] = jnp.zeros(hf_sc.shape, F32)
        hb_sc[...] = jnp.zeros(hb_sc.shape, F32)
        nl = -lam_ref[...]
        sp_sc[...] = jnp.maximum(nl, 0.0) + jnp.log(1.0 + jnp.exp(-jnp.abs(nl)))

    first, last = i == 0, i == nt - 1
    _lru_prepare(xe_sc, xf_ref, pf_ref, nf_ref, first, last, cw_ref, cb_ref, gw_ref, gb_ref,
                 sp_sc, af_sc, uf_sc, 0, ts, jnp.where(first, 0, -1))
    _lru_prepare(xe_sc, xb_ref, pb_ref, nb_ref, last, first, cw_ref, cb_ref, gw_ref, gb_ref,
                 sp_sc, ab_sc, ub_sc, 1, ts, jnp.where(first, ts - 1, -1))

    def step(k, carry):
        hf, hb = carry
        kb = ts - 1 - k
        hf = af_sc[pl.ds(k, 1), :] * hf + uf_sc[pl.ds(k, 1), :]
        hb = ab_sc[pl.ds(kb, 1), :] * hb + ub_sc[pl.ds(kb, 1), :]
        yf_ref[0, pl.ds(k, 1), :] = hf
        yb_ref[0, pl.ds(kb, 1), :] = hb
        return hf, hb

    hf, hb = lax.fori_loop(0, ts, step, (hf_sc[...], hb_sc[...]), unroll=8)
    hf_sc[...] = hf
    hb_sc[...] = hb


def lru_scan(z, conv_w, conv_b, gate_w, gate_b, lam, *, ts):
    b, s, _ = z.shape
    w = LRU_WIDTH
    h = LRU_HALO
    nt = s // ts
    rpb = ts // h
    nhb = s // h
    fwd = lambda bi, i: (bi, i, 0)
    bwd = lambda bi, i: (bi, nt - 1 - i, 0)
    fwd_prev = lambda bi, i: (bi, jnp.maximum(i * rpb - 1, 0), 0)
    fwd_next = lambda bi, i: (bi, jnp.minimum((i + 1) * rpb, nhb - 1), 0)
    bwd_prev = lambda bi, i: (bi, jnp.maximum((nt - 1 - i) * rpb - 1, 0), 0)
    bwd_next = lambda bi, i: (bi, jnp.minimum((nt - i) * rpb, nhb - 1), 0)
    full = lambda *shape: pl.BlockSpec(shape, lambda bi, i: (0,) * len(shape))
    return pl.pallas_call(
        functools.partial(_lru_kernel, ts=ts),
        grid=(b, nt),
        in_specs=[
            pl.BlockSpec((1, ts, w), fwd), pl.BlockSpec((1, h, w), fwd_prev),
            pl.BlockSpec((1, h, w), fwd_next),
            pl.BlockSpec((1, ts, w), bwd), pl.BlockSpec((1, h, w), bwd_prev),
            pl.BlockSpec((1, h, w), bwd_next),
            full(4, w), full(1, w), full(2, LRU_BLOCKS, LRU_BW, 2 * LRU_BW), full(4, w), full(2, w),
        ],
        out_specs=[pl.BlockSpec((1, ts, w), fwd), pl.BlockSpec((1, ts, w), bwd)],
        out_shape=[jax.ShapeDtypeStruct((b, s, w), F32)] * 2,
        scratch_shapes=[
            pltpu.VMEM((ts + 2 * h, w), F32),
            pltpu.VMEM((ts, w), F32), pltpu.VMEM((ts, w), F32),
            pltpu.VMEM((ts, w), F32), pltpu.VMEM((ts, w), F32),
            pltpu.VMEM((1, w), F32), pltpu.VMEM((1, w), F32), pltpu.VMEM((2, w), F32),
        ],
        compiler_params=_params("parallel", "arbitrary"),
        name="lru_scan",
    )(z, z, z, z, z, z, conv_w, conv_b, gate_w, gate_b, lam)


RWKV_HALO = 8


def _rwkv_shifted(x_ref, p_ref, n_ref, g_ref, sc_ref, sh_ref, he_sc, is_first, is_last, ts):
    hl = RWKV_HALO
    d = D_MODEL
    g, sc, sh = g_ref[...], sc_ref[0], sh_ref[0]
    he_sc[pl.ds(hl, ts), :] = _modnorm(x_ref[0], g, sc, sh)

    @pl.when(is_first)
    def _():
        he_sc[pl.ds(0, hl), :] = jnp.zeros((hl, d), F32)

    @pl.when(jnp.logical_not(is_first))
    def _():
        he_sc[pl.ds(0, hl), :] = _modnorm(p_ref[0], g, sc, sh)

    @pl.when(is_last)
    def _():
        he_sc[pl.ds(hl + ts, hl), :] = jnp.zeros((hl, d), F32)

    @pl.when(jnp.logical_not(is_last))
    def _():
        he_sc[pl.ds(hl + ts, hl), :] = _modnorm(n_ref[0], g, sc, sh)

    h = he_sc[pl.ds(hl, ts), :]
    xx = 0.5 * (he_sc[pl.ds(hl - 1, ts), :] + he_sc[pl.ds(hl + 1, ts), :]) - h
    return h, xx


def _rwkv_proj_kernel(x_ref, p_ref, n_ref, g_ref, sc_ref, sh_ref, mu_ref, w_ref, o_ref,
                      he_sc, xx_sc, *, ts):
    i, m = pl.program_id(1), pl.program_id(2)

    @pl.when(m == 0)
    def _():
        _, xx = _rwkv_shifted(x_ref, p_ref, n_ref, g_ref, sc_ref, sh_ref, he_sc,
                              i == 0, i == pl.num_programs(1) - 1, ts)
        xx_sc[...] = xx

    xs = he_sc[pl.ds(RWKV_HALO, ts), :] + xx_sc[...] * mu_ref[0]
    o_ref[0] = jnp.dot(xs.astype(BF16), w_ref[0], preferred_element_type=F32).astype(o_ref.dtype)


def _halo_specs(ts, s, halo, d):
    rpb, nhb = ts // halo, s // halo
    prev = lambda bi, i, *_: (bi, jnp.maximum(i * rpb - 1, 0), 0)
    nxt = lambda bi, i, *_: (bi, jnp.minimum((i + 1) * rpb, nhb - 1), 0)
    return pl.BlockSpec((1, halo, d), prev), pl.BlockSpec((1, halo, d), nxt)


def rwkv_proj(x, g, scale, shift, mu, w_in, *, ts):
    b, s, d = x.shape
    prev, nxt = _halo_specs(ts, s, RWKV_HALO, d)
    return pl.pallas_call(
        functools.partial(_rwkv_proj_kernel, ts=ts),
        grid=(b, s // ts, 4),
        in_specs=[
            pl.BlockSpec((1, ts, d), lambda bi, i, m: (bi, i, 0)), prev, nxt,
            pl.BlockSpec((1, d), lambda bi, i, m: (0, 0)),
            pl.BlockSpec((1, 1, d), lambda bi, i, m: (bi, 0, 0)),
            pl.BlockSpec((1, 1, d), lambda bi, i, m: (bi, 0, 0)),
            pl.BlockSpec((1, 1, d), lambda bi, i, m: (m, 0, 0)),
            pl.BlockSpec((1, d, d), lambda bi, i, m: (m, 0, 0)),
        ],
        out_specs=pl.BlockSpec((1, ts, d), lambda bi, i, m: (bi, i, m)),
        out_shape=jax.ShapeDtypeStruct((b, s, 4 * d), BF16),
        scratch_shapes=[pltpu.VMEM((ts + 2 * RWKV_HALO, d), F32), pltpu.VMEM((ts, d), F32)],
        compiler_params=_params("parallel", "parallel", "arbitrary"),
        name="rwkv_proj",
    )(x, x, x, g, scale, shift, mu.reshape(6, 1, d), w_in)


def _neg_softplus_neg(x):
    return -(jnp.maximum(-x, 0.0) + jnp.log(1.0 + jnp.exp(-jnp.abs(x))))


def _rwkv_lora_kernel(x_ref, p_ref, n_ref, g_ref, sc_ref, sh_ref, mu_ref, w0_ref, w1_ref, w2_ref,
                      a0_ref, a1_ref, a2_ref, lw_ref, a_ref, he_sc, *, ts):
    i = pl.program_id(1)
    h, xx = _rwkv_shifted(x_ref, p_ref, n_ref, g_ref, sc_ref, sh_ref, he_sc,
                          i == 0, i == pl.num_programs(1) - 1, ts)
    xw = (h + xx * mu_ref[4]).astype(BF16)
    xa = (h + xx * mu_ref[5]).astype(BF16)
    for dr in range(2):
        t1 = jnp.tanh(jnp.dot(xw, w1_ref[dr], preferred_element_type=F32))
        wl = w0_ref[dr] + jnp.dot(t1.astype(BF16), w2_ref[dr], preferred_element_type=F32)
        lw_ref[dr, 0] = -jnp.exp(_neg_softplus_neg(wl) - 0.5)
        t2 = jnp.dot(xa, a1_ref[dr], preferred_element_type=F32)
        al = a0_ref[dr] + jnp.dot(t2.astype(BF16), a2_ref[dr], preferred_element_type=F32)
        a_ref[dr, 0] = jax.nn.sigmoid(al).astype(a_ref.dtype)


def rwkv_lora(x, g, scale, shift, mu, w0, w1, w2, a0, a1, a2, *, ts):
    b, s, d = x.shape
    lp = RWKV_LORA_PAD
    prev, nxt = _halo_specs(ts, s, RWKV_HALO, d)
    full = lambda *shape: pl.BlockSpec(shape, lambda bi, i: (0,) * len(shape))
    out = pl.BlockSpec((2, 1, ts, d), lambda bi, i: (0, bi, i, 0))
    return pl.pallas_call(
        functools.partial(_rwkv_lora_kernel, ts=ts),
        grid=(b, s // ts),
        in_specs=[
            pl.BlockSpec((1, ts, d), lambda bi, i: (bi, i, 0)), prev, nxt,
            full(1, d),
            pl.BlockSpec((1, 1, d), lambda bi, i: (bi, 0, 0)),
            pl.BlockSpec((1, 1, d), lambda bi, i: (bi, 0, 0)),
            full(6, 1, d), full(2, 1, d), full(2, d, lp), full(2, lp, d),
            full(2, 1, d), full(2, d, lp), full(2, lp, d),
        ],
        out_specs=[out, out],
        out_shape=[jax.ShapeDtypeStruct((2, b, s, d), F32), jax.ShapeDtypeStruct((2, b, s, d), BF16)],
        scratch_shapes=[pltpu.VMEM((ts + 2 * RWKV_HALO, d), F32)],
        compiler_params=_params("parallel", "parallel"),
        name="rwkv_lora",
    )(x, x, x, g, scale, shift, mu.reshape(6, 1, d), w0.reshape(2, 1, d), w1, w2,
      a0.reshape(2, 1, d), a1, a2)


def _hilo(x):
    hi = x.astype(BF16)
    return hi, (x - hi.astype(F32)).astype(BF16)


def _dot2(x, w):
    hi, lo = _hilo(x)
    return (jnp.dot(hi, w, preferred_element_type=F32) + jnp.dot(lo, w, preferred_element_type=F32))


def _nt(x, y):
    return lax.dot_general(x, y, (((1,), (1,)), ((), ())), preferred_element_type=F32)


def _bdot(x, y):
    return jnp.dot(x.astype(BF16), y.astype(BF16), preferred_element_type=F32)


def _rwkv_masks(reverse):
    t = RWKV_CHUNK
    n = RWKV_N
    lane_t = lax.broadcasted_iota(jnp.int32, (t, LANES), 1)
    row_t = lax.broadcasted_iota(jnp.int32, (t, LANES), 0)
    col_t = lane_t % n
    r128 = lax.broadcasted_iota(jnp.int32, (LANES, LANES), 0)
    c128 = lax.broadcasted_iota(jnp.int32, (LANES, LANES), 1)
    rt = lax.broadcasted_iota(jnp.int32, (t, t), 0)
    ct = lax.broadcasted_iota(jnp.int32, (t, t), 1)
    if reverse:
        strict2, incl2, tri = col_t > row_t, col_t >= row_t, ct >= rt
    else:
        strict2, incl2, tri = col_t < row_t, col_t <= row_t, ct <= rt
    same_head = (r128 // n) == (c128 // n)
    levels = []
    sz = 1
    while sz < t:
        hi_r, lo_c = (r128 % (2 * sz)) >= sz, (c128 % (2 * sz)) < sz
        if reverse:
            off = jnp.logical_and(jnp.logical_not(hi_r), jnp.logical_not(lo_c))
        else:
            off = jnp.logical_and(hi_r, lo_c)
        levels.append(jnp.logical_and(off, (r128 // (2 * sz)) == (c128 // (2 * sz))))
        sz *= 2
    return dict(m1=lane_t < n, strict2=strict2, incl2=incl2,
                tri=jnp.where(tri, 1.0, 0.0).astype(BF16), same_head=same_head,
                ones_bd=jnp.where(same_head, 1.0, 0.0).astype(BF16),
                eye=jnp.where(r128 == c128, 1.0, 0.0), levels=levels)


def _each(f, *lists):
    return [f(*xs) for xs in zip(*lists)]


def _rwkv_chunk_terms(lw, r, k, v, a, kk_scale, ka_scale, rk_scale, mk, reverse):
    t = RWKV_CHUNK
    m1, strict2, incl2, tri, same_head = mk['m1'], mk['strict2'], mk['incl2'], mk['tri'], mk['same_head']

    def sel(top, bot):
        return jnp.where(m1, top, bot)

    def stack(x):
        return jnp.concatenate([jnp.where(m1, x, 0.0), jnp.where(m1, 0.0, x)], axis=0)

    kk = _each(jnp.multiply, k, kk_scale)
    kd = _each(lambda k_, a_, s_: k_ * (1.0 + (a_ - 1.0) * s_), k, a, ka_scale)
    seg = _each(lambda kk_, r_, kd_, s_: _dot2(
        jnp.concatenate([kk_ * kk_, r_ * kd_ * s_], axis=0), mk['ones_bd']), kk, r, kd, rk_scale)
    kk = _each(lambda kk_, sg: kk_ / jnp.maximum(jnp.sqrt(sg[:t]), 1e-12), kk, seg)
    bonus = _each(lambda sg, v_: sg[t:] * v_, seg, v)

    c = _each(lambda lw_: _dot2_left(tri, lw_), lw)
    e_pos = _each(jnp.exp, c)
    e_neg = _each(lambda c_: jnp.exp(-c_), c)
    at = _each(lambda kk_, c_, lw_: -kk_ * jnp.exp(c_ - lw_), kk, c, lw)
    rt_ = _each(jnp.multiply, r, e_pos)
    bt = _each(lambda kk_, a_, e_: kk_ * a_ * e_, kk, a, e_neg)
    kt = _each(jnp.multiply, kd, e_neg)
    big = _each(lambda at_, rt__, bt_, kt_: _nt(
        jnp.concatenate([at_, rt__], axis=0).astype(BF16),
        jnp.concatenate([stack(bt_), stack(kt_)], axis=0).astype(BF16)), at, rt_, bt, kt)
    aak = _each(lambda b_: jnp.where(strict2, b_[:t, LANES:], 0.0), big)
    rb = _each(lambda b_: jnp.where(incl2, b_[t:, :LANES], 0.0), big)
    rk = _each(lambda b_: jnp.where(incl2, b_[t:, LANES:], 0.0), big)

    abd = _each(lambda b_: stack(jnp.where(strict2, b_[:t, :LANES], 0.0)), big)
    dinv = _each(lambda ab: mk['eye'] + jnp.where(mk['levels'][0], ab, 0.0), abd)
    for off in mk['levels'][1:]:
        dinv_b = _each(lambda d_: d_.astype(BF16), dinv)
        half = _each(lambda db, ab: jnp.dot(db, jnp.where(off, ab, 0.0).astype(BF16),
                                            preferred_element_type=F32), dinv_b, abd)
        dinv = _each(lambda d_, h_, db: d_ + jnp.dot(h_.astype(BF16), db, preferred_element_type=F32),
                     dinv, half, dinv_b)

    vst = _each(stack, v)
    aakv = _each(_bdot, aak, vst)
    x = _each(lambda d_, at_, av: _bdot(d_, jnp.concatenate(
        [jnp.concatenate([at_, av], axis=1)] * 2, axis=0)), dinv, at, aakv)
    atp = _each(lambda x_: sel(x_[:t, :LANES], x_[t:, :LANES]), x)
    u0 = _each(lambda x_: sel(x_[:t, LANES:], x_[t:, LANES:]), x)
    y = _each(lambda rb_, p_, u_: _bdot(rb_, jnp.concatenate([stack(p_), stack(u_)], axis=1)),
              rb, atp, u0)
    rkv = _each(_bdot, rk, vst)
    rp = _each(lambda rt__, y_: rt__ + y_[:, :LANES], rt_, y)
    o0 = _each(lambda y_, q_: y_[:, LANES:] + q_, y, rkv)
    gmt = _each(lambda p_, bt_: jnp.where(same_head, _bdot(p_.T, bt_), 0.0), atp, bt)
    zty = _each(lambda u_, v_, bt_, kt_: jnp.where(same_head, _bdot(
        jnp.concatenate([u_, v_], axis=0).T, jnp.concatenate([bt_, kt_], axis=0)), 0.0),
        u0, v, bt, kt)
    decay_row = _each(lambda e_: e_[0:1, :] if reverse else e_[t - 1:t, :], e_pos)
    return list(zip(rp, o0, gmt, zty, decay_row, bonus))


def _dot2_left(w, x):
    hi, lo = _hilo(x)
    return (jnp.dot(w, hi, preferred_element_type=F32) + jnp.dot(w, lo, preferred_element_type=F32))


def _rwkv_scan_kernel(lw_ref, a_ref, r_ref, k_ref, v_ref, kk_ref, ka_ref, rk_ref, o_ref, bo_ref,
                      s_sc, *, ts, npairs, reverse):
    @pl.when(pl.program_id(2) == 0)
    def _():
        s_sc[...] = jnp.zeros(s_sc.shape, F32)

    t = RWKV_CHUNK
    nch = ts // t
    order = range(nch - 1, -1, -1) if reverse else range(nch)
    mk = _rwkv_masks(reverse)
    chains = [(ci, hp) for ci in order for hp in range(npairs)]
    rows = lambda ci: slice(ci * t, (ci + 1) * t)
    cols = lambda hp: slice(hp * LANES, (hp + 1) * LANES)
    blk = lambda ref: [ref[0, rows(ci), cols(hp)].astype(F32) for ci, hp in chains]
    blk4 = lambda ref: [ref[0, 0, rows(ci), cols(hp)].astype(F32) for ci, hp in chains]
    par = lambda ref: [ref[:, cols(hp)] for _, hp in chains]
    terms = dict(zip(chains, _rwkv_chunk_terms(
        blk4(lw_ref), blk(r_ref), blk(k_ref), blk(v_ref), blk4(a_ref),
        par(kk_ref), par(ka_ref), par(rk_ref), mk, reverse)))
    s = [s_sc[hp] for hp in range(npairs)]
    for ci in order:
        for hp in range(npairs):
            rp, o0, gmt, zty, decay_row, bonus = terms[ci, hp]
            s_hi, s_lo = _hilo(s[hp])
            rp_b = rp.astype(BF16)
            o_ref[0, rows(ci), cols(hp)] = o0 + _nt(rp_b, s_hi) + _nt(rp_b, s_lo)
            bo_ref[0, rows(ci), cols(hp)] = bonus.astype(bo_ref.dtype)
            gmt_b = gmt.astype(BF16)
            s[hp] = (s[hp] + jnp.dot(s_hi, gmt_b, preferred_element_type=F32)
                     + jnp.dot(s_lo, gmt_b, preferred_element_type=F32) + zty) * decay_row
    for hp in range(npairs):
        s_sc[hp] = s[hp]


def rwkv_scan(lw, a, rkvg, k_k, k_a, r_k, dr, *, ts, npairs):
    _, b, s, d = lw.shape
    nt = s // ts
    wd = npairs * LANES
    ng = d // wd
    reverse = dr == 1
    tmap = (lambda i: nt - 1 - i) if reverse else (lambda i: i)
    dspec = pl.BlockSpec((1, 1, ts, wd), lambda bi, g, i: (dr, bi, tmap(i), g))
    col = lambda m: pl.BlockSpec((1, ts, wd), lambda bi, g, i: (bi, tmap(i), m * ng + g))
    par = pl.BlockSpec((1, wd), lambda bi, g, i: (0, g))
    out = pl.BlockSpec((1, ts, wd), lambda bi, g, i: (bi, tmap(i), g))
    return pl.pallas_call(
        functools.partial(_rwkv_scan_kernel, ts=ts, npairs=npairs, reverse=reverse),
        grid=(b, ng, nt),
        in_specs=[dspec, dspec, col(0), col(1), col(2), par, par, par],
        out_specs=[out, out],
        out_shape=[jax.ShapeDtypeStruct((b, s, d), F32), jax.ShapeDtypeStruct((b, s, d), BF16)],
        scratch_shapes=[pltpu.VMEM((npairs, LANES, LANES), F32)],
        compiler_params=_params("parallel", "parallel", "arbitrary"),
        name="rwkv_scan_bwd" if reverse else "rwkv_scan_fwd",
    )(lw, a, rkvg, rkvg, rkvg, k_k, k_a, r_k)


def _rwkv_out_kernel(of_ref, ob_ref, bf_ref, bb_ref, g_ref, lg_ref, lb_ref, w_ref, x_ref, gm_ref,
                     gp_ref, o_ref, a_sc):
    r128 = lax.broadcasted_iota(jnp.int32, (LANES, LANES), 0)
    c128 = lax.broadcasted_iota(jnp.int32, (LANES, LANES), 1)
    avg = jnp.where((r128 // RWKV_N) == (c128 // RWKV_N), 1.0 / RWKV_N, 0.0).astype(BF16)
    for c in range(D_MODEL // LANES):
        cols = slice(c * LANES, (c + 1) * LANES)
        o = of_ref[0, :, cols] + ob_ref[0, :, cols]
        cen = o - _dot2(o, avg)
        var = _dot2(cen * cen, avg)
        on = cen * lax.rsqrt(var + RWKV_GN_EPS)
        y = (on * lg_ref[:, cols] + lb_ref[:, cols]
             + bf_ref[0, :, cols].astype(F32) + bb_ref[0, :, cols].astype(F32))
        a_sc[:, cols] = (y * _silu(g_ref[0, :, cols].astype(F32))).astype(BF16)
    _finish(a_sc[...], w_ref, x_ref, gm_ref, gp_ref, o_ref)


def rwkv_out(o_f, o_b, bo_f, bo_b, rkvg, ln_g, ln_b, w, x, gate_mod, g_post, *, ts):
    b, s, d = x.shape
    row = lambda bi, i: (bi, i, 0)
    vec = pl.BlockSpec((1, d), lambda bi, i: (0, 0))
    return pl.pallas_call(
        _rwkv_out_kernel,
        grid=(b, s // ts),
        in_specs=[pl.BlockSpec((1, ts, d), row)] * 4 + [
            pl.BlockSpec((1, ts, d), lambda bi, i: (bi, i, 3)), vec, vec,
            pl.BlockSpec((d, d), lambda bi, i: (0, 0)),
            pl.BlockSpec((1, ts, d), row),
            pl.BlockSpec((1, 1, d), lambda bi, i: (bi, 0, 0)),
            vec,
        ],
        out_specs=pl.BlockSpec((1, ts, d), row),
        out_shape=jax.ShapeDtypeStruct((b, s, d), F32),
        scratch_shapes=[pltpu.VMEM((ts, d), BF16)],
        compiler_params=_params("parallel", "parallel"),
        name="rwkv_out",
    )(o_f, o_b, bo_f, bo_b, rkvg, ln_g, ln_b, w, x, gate_mod, g_post)


def _prep_weights(p):
    d = D_MODEL
    w = {}
    wi = p['mla_w_in'][0]
    q_lat, kv_lat, k_pe, gate = jnp.split(
        wi, [MLA_Q_RANK, MLA_Q_RANK + MLA_KV_RANK, MLA_Q_RANK + MLA_KV_RANK + MLA_ROPE], axis=1)
    w['mla_in'] = jnp.concatenate(
        [gate, q_lat, kv_lat, k_pe, jnp.zeros((d, LANES - MLA_ROPE), F32)], axis=1).astype(BF16)
    wq = p['mla_w_q_up'][0].reshape(MLA_Q_RANK, MLA_HEADS, MLA_NOPE + MLA_ROPE)
    wq = jnp.pad(wq, ((0, 0), (0, 0), (0, MLA_QK_PAD - MLA_NOPE - MLA_ROPE)))
    w['mla_q'] = wq.reshape(MLA_Q_RANK, MLA_HEADS * MLA_QK_PAD).astype(BF16)
    w['mla_kv'] = p['mla_w_kv_up'][0].astype(BF16)
    w['mla_out'] = p['mla_w_out'][0].astype(BF16)
    w['diff_in'] = p['diff_w_in'][0].astype(BF16)
    w['diff_out'] = p['diff_w_out'][0].astype(BF16)
    w['lru_in'] = p['lru_w_in'][0].astype(BF16)
    w['lru_out'] = p['lru_w_out'][0].astype(BF16)
    gw = p['lru_gate_w'][0]
    w['lru_gate'] = jnp.concatenate([gw[:, 0], gw[:, 1]], axis=-1).astype(BF16)
    w['rwkv_in'] = p['rwkv_w_in'][0].astype(BF16)
    pad = RWKV_LORA_PAD - RWKV_LORA
    for nm in ('w1', 'a1'):
        w['rwkv_' + nm] = jnp.pad(p['rwkv_' + nm][0], ((0, 0), (0, 0), (0, pad))).astype(BF16)
    for nm in ('w2', 'a2'):
        w['rwkv_' + nm] = jnp.pad(p['rwkv_' + nm][0], ((0, 0), (0, pad), (0, 0))).astype(BF16)
    w['rwkv_out'] = p['rwkv_w_out'][0].astype(BF16)
    return w


def _mla_layer(x, pre_g, scale, shift, gate_mod, post_g, p, w, tabs):
    mla_tabs = tabs['mla']
    z = modmm(x, pre_g, scale, shift, w['mla_in'], ts=512, tn=640,
              rope=(24, 25, MLA_ROPE // 2, 1, 0), tabs=mla_tabs)
    q = normmm(z, 4, p['mla_q_norm_g'], w['mla_q'], ts=1024, tn=512,
               rope=(0, 2 * MLA_HEADS, MLA_ROPE // 2, 2, 1), tabs=mla_tabs)
    kv = normmm(z, 5, p['mla_kv_norm_g'], w['mla_kv'], ts=1024, tn=512)
    o = mla_attention(q, kv, z, 24, tq=512, tk=1024)
    return outmm([o], z, 0, w['mla_out'], x, gate_mod, post_g, ts=256)


def _diff_layer(x, layer_idx, pre_g, scale, shift, gate_mod, post_g, p, w, tabs):
    z = modmm(x, pre_g, scale, shift, w['diff_in'], ts=512, tn=512,
              rope=(0, 2 * DIFF_HEADS, DIFF_ROT // 2, 1, 0), tabs=tabs['diff'])
    lam_init = 0.8 - 0.6 * math.exp(-0.3 * layer_idx)
    o = diff_attention(z, p['diff_lambda'][0], p['diff_subln_g'][0], lam_init, tq=256, tk=1024)
    return outmm([o], z, 3, w['diff_out'], x, gate_mod, post_g, ts=256)


def _lru_layer(x, pre_g, scale, shift, gate_mod, post_g, p, w):
    z = modmm(x, pre_g, scale, shift, w['lru_in'], ts=512, tn=512)
    y_f, y_b = lru_scan(z, p['lru_conv_w'][0], p['lru_conv_b'], w['lru_gate'],
                        p['lru_gate_b'][0].reshape(4, LRU_WIDTH), p['lru_lambda'][0], ts=256)
    return outmm([y_f, y_b], z, 1, w['lru_out'], x, gate_mod, post_g, ts=256)


def _rwkv_layer(x, pre_g, scale, shift, gate_mod, post_g, p, w):
    mu = p['rwkv_mu'][0]
    rkvg = rwkv_proj(x, pre_g, scale, shift, mu, w['rwkv_in'], ts=512)
    lw, a = rwkv_lora(x, pre_g, scale, shift, mu, p['rwkv_w0'][0], w['rwkv_w1'], w['rwkv_w2'],
                      p['rwkv_a0'][0], w['rwkv_a1'], w['rwkv_a2'], ts=256)
    k_k = p['rwkv_k_k']
    k_a = p['rwkv_k_a']
    r_k = p['rwkv_r_k'].reshape(1, D_MODEL)
    o_f, bo_f = rwkv_scan(lw, a, rkvg, k_k, k_a, r_k, 0, ts=256, npairs=4)
    o_b, bo_b = rwkv_scan(lw, a, rkvg, k_k, k_a, r_k, 1, ts=256, npairs=4)
    return rwkv_out(o_f, o_b, bo_f, bo_b, rkvg, p['rwkv_ln_g'], p['rwkv_ln_b'], w['rwkv_out'],
                    x, gate_mod, post_g, ts=256)


def _trunk(x, mods, p, w):
    b, s, d = x.shape
    tabs = {
        'mla': _rope_tables(s, 1, LANES, MLA_ROPE, MLA_THETA, 0.0),
        'diff': _rope_tables(s, LANES // DIFF_HD, DIFF_HD, DIFF_ROT, ROPE_THETA, 1.0),
    }
    for i in range(DEPTH):
        shift, scale, gate_mod = (mods[i, :, k * d:(k + 1) * d].reshape(b, 1, d) for k in range(3))
        pre_g = p['norm_pre_g'][i:i + 1]
        post_g = p['norm_post_g'][i:i + 1]
        args = (pre_g, scale, shift, gate_mod, post_g, p, w)
        if i % 4 == 0:
            x = _mla_layer(x, *args, tabs)
        elif i % 4 == 1:
            x = _diff_layer(x, i, *args, tabs)
        elif i % 4 == 2:
            x = _lru_layer(x, *args)
        else:
            x = _rwkv_layer(x, *args)
    return x


def kernel(x_prompt, x_sample, c_prompt, c_sample, ada_w, ada_b, norm_pre_g, norm_post_g, mla_w_in, mla_q_norm_g, mla_kv_norm_g, mla_w_q_up, mla_w_kv_up, mla_w_out, diff_w_in, diff_lambda, diff_subln_g, diff_w_out, lru_w_in, lru_conv_w, lru_conv_b, lru_gate_w, lru_gate_b, lru_lambda, lru_w_out, rwkv_mu, rwkv_w_in, rwkv_w0, rwkv_w1, rwkv_w2, rwkv_a0, rwkv_a1, rwkv_a2, rwkv_k_k, rwkv_k_a, rwkv_r_k, rwkv_ln_g, rwkv_ln_b, rwkv_w_out):
    p = dict(
        norm_pre_g=norm_pre_g, norm_post_g=norm_post_g,
        mla_w_in=mla_w_in, mla_q_norm_g=mla_q_norm_g, mla_kv_norm_g=mla_kv_norm_g,
        mla_w_q_up=mla_w_q_up, mla_w_kv_up=mla_w_kv_up, mla_w_out=mla_w_out,
        diff_w_in=diff_w_in, diff_lambda=diff_lambda, diff_subln_g=diff_subln_g,
        diff_w_out=diff_w_out,
        lru_w_in=lru_w_in, lru_conv_w=lru_conv_w, lru_conv_b=lru_conv_b, lru_gate_w=lru_gate_w,
        lru_gate_b=lru_gate_b, lru_lambda=lru_lambda, lru_w_out=lru_w_out,
        rwkv_mu=rwkv_mu, rwkv_w_in=rwkv_w_in, rwkv_w0=rwkv_w0, rwkv_w1=rwkv_w1, rwkv_w2=rwkv_w2,
        rwkv_a0=rwkv_a0, rwkv_a1=rwkv_a1, rwkv_a2=rwkv_a2, rwkv_k_k=rwkv_k_k, rwkv_k_a=rwkv_k_a,
        rwkv_r_k=rwkv_r_k, rwkv_ln_g=rwkv_ln_g, rwkv_ln_b=rwkv_ln_b, rwkv_w_out=rwkv_w_out,
    )
    w = _prep_weights(p)
    nb = x_prompt.shape[0]
    mods = ada_mod(jnp.concatenate([c_prompt, c_sample], axis=0), ada_w, ada_b)
    y_prompt = _trunk(x_prompt, mods[:, :nb], p, w)
    y_sample = _trunk(x_sample, mods[:, nb:], p, w)
    return (y_prompt, y_sample)
```

```python
import functools
import math

import jax
import jax.numpy as jnp
from jax import lax
from jax.experimental import pallas as pl
from jax.experimental.pallas import tpu as pltpu

F32 = jnp.float32
BF16 = jnp.bfloat16

D_MODEL = 2048
DEPTH = 4
NORM_EPS = 1e-6
LANES = 128
VMEM_LIMIT = 56 * 1024 * 1024

MLA_HEADS = 16
MLA_Q_RANK = 512
MLA_KV_RANK = 512
MLA_NOPE = 128
MLA_ROPE = 64
MLA_V = 128
MLA_THETA = 10000.0
MLA_QK_PAD = 256

DIFF_HEADS = 16
DIFF_HD = 64
DIFF_ROT = 16
ROPE_THETA = 500000.0
DIFF_SUBLN_EPS = 1e-5

LRU_WIDTH = 2048
LRU_BLOCKS = 16
LRU_BW = 128
LRU_C = 8.0
LRU_HALO = 16

RWKV_N = 64
RWKV_HEADS = 32
RWKV_LORA = 96
RWKV_LORA_PAD = 128
RWKV_GN_EPS = 64e-5
RWKV_CHUNK = 64


def _params(*sem):
    return pltpu.CompilerParams(dimension_semantics=sem, vmem_limit_bytes=VMEM_LIMIT)


def _silu(x):
    return x * jax.nn.sigmoid(x)


def _rope128(y, c, s1, s2, shift):
    return y * c + pltpu.roll(y, LANES - shift, 1) * s1 + pltpu.roll(y, shift, 1) * s2


def _rope_tables(seq, heads_per_chunk, head_dim, rot_dim, theta, pad_cos):
    half = rot_dim // 2
    inv = 1.0 / (theta ** (jnp.arange(0, rot_dim, 2, dtype=F32) / rot_dim))
    ang = jnp.arange(seq, dtype=F32)[:, None] * inv[None, :]
    cos, sin = jnp.cos(ang), jnp.sin(ang)
    rest = head_dim - rot_dim
    c = jnp.concatenate([cos, cos, jnp.full((seq, rest), pad_cos, F32)], axis=-1)
    s1 = jnp.concatenate([-sin, jnp.zeros((seq, half + rest), F32)], axis=-1)
    s2 = jnp.concatenate([jnp.zeros((seq, half), F32), sin, jnp.zeros((seq, rest), F32)], axis=-1)
    tile = lambda t: jnp.tile(t, (1, heads_per_chunk))
    return tile(c), tile(s1), tile(s2)


def _ada_kernel(c_ref, w_ref, b_ref, o_ref):
    cs = _silu(c_ref[...]).astype(BF16)
    o_ref[0] = jnp.dot(cs, w_ref[0].astype(BF16), preferred_element_type=F32) + b_ref[0]


def ada_mod(c, ada_w, ada_b):
    nb, d = c.shape
    depth, _, n = ada_w.shape
    tn = 512
    return pl.pallas_call(
        _ada_kernel,
        grid=(depth, n // tn),
        in_specs=[
            pl.BlockSpec((nb, d), lambda l, j: (0, 0)),
            pl.BlockSpec((1, d, tn), lambda l, j: (l, 0, j)),
            pl.BlockSpec((1, 1, tn), lambda l, j: (l, 0, j)),
        ],
        out_specs=pl.BlockSpec((1, nb, tn), lambda l, j: (l, 0, j)),
        out_shape=jax.ShapeDtypeStruct((depth, nb, n), F32),
        compiler_params=_params("parallel", "parallel"),
        name="ada_mod",
    )(c, ada_w, ada_b.reshape(depth, 1, n))


def _modnorm(x, g, sc, sh):
    ms = jnp.mean(x * x, axis=-1, keepdims=True)
    return x * lax.rsqrt(ms + NORM_EPS) * g * (1.0 + sc) + sh


def _store_with_rope(y, o_ref, j, tn, rope, tabs):
    if rope is None:
        o_ref[0] = y.astype(o_ref.dtype)
        return
    lo, hi, shift, stride, phase = rope
    c_ref, s1_ref, s2_ref = tabs
    cpt = tn // LANES
    for c in range(cpt):
        gc = j * cpt + c
        yc = y[:, c * LANES:(c + 1) * LANES]
        is_rope = jnp.logical_and(jnp.logical_and(gc >= lo, gc < hi), gc % stride == phase)

        @pl.when(is_rope)
        def _(yc=yc, c=c):
            o_ref[0, :, c * LANES:(c + 1) * LANES] = _rope128(
                yc, c_ref[...], s1_ref[...], s2_ref[...], shift).astype(o_ref.dtype)

        @pl.when(jnp.logical_not(is_rope))
        def _(yc=yc, c=c):
            o_ref[0, :, c * LANES:(c + 1) * LANES] = yc.astype(o_ref.dtype)


def _modmm_kernel(x_ref, g_ref, sc_ref, sh_ref, w_ref, *rest, tn, rope):
    tabs, (o_ref, h_ref) = rest[:-2], rest[-2:]
    j = pl.program_id(2)

    @pl.when(j == 0)
    def _():
        h_ref[...] = _modnorm(x_ref[0], g_ref[...], sc_ref[0], sh_ref[0]).astype(BF16)

    y = jnp.dot(h_ref[...], w_ref[...], preferred_element_type=F32)
    _store_with_rope(y, o_ref, j, tn, rope, tabs)


def modmm(x, g, scale, shift, w, *, ts, tn, out_dtype=BF16, rope=None, tabs=()):
    b, s, d = x.shape
    n = w.shape[1]
    tab_specs = [pl.BlockSpec((ts, LANES), lambda bi, i, j: (i, 0)) for _ in tabs]
    return pl.pallas_call(
        functools.partial(_modmm_kernel, tn=tn, rope=rope),
        grid=(b, s // ts, n // tn),
        in_specs=[
            pl.BlockSpec((1, ts, d), lambda bi, i, j: (bi, i, 0)),
            pl.BlockSpec((1, d), lambda bi, i, j: (0, 0)),
            pl.BlockSpec((1, 1, d), lambda bi, i, j: (bi, 0, 0)),
            pl.BlockSpec((1, 1, d), lambda bi, i, j: (bi, 0, 0)),
            pl.BlockSpec((d, tn), lambda bi, i, j: (0, j)),
        ] + tab_specs,
        out_specs=pl.BlockSpec((1, ts, tn), lambda bi, i, j: (bi, i, j)),
        out_shape=jax.ShapeDtypeStruct((b, s, n), out_dtype),
        scratch_shapes=[pltpu.VMEM((ts, d), BF16)],
        compiler_params=_params("parallel", "parallel", "arbitrary"),
        name="modmm",
    )(x, g, scale, shift, w, *tabs)


def _normmm_kernel(x_ref, g_ref, w_ref, *rest, tn, rope):
    tabs, (o_ref, h_ref) = rest[:-2], rest[-2:]
    j = pl.program_id(2)

    @pl.when(j == 0)
    def _():
        x = x_ref[0].astype(F32)
        ms = jnp.mean(x * x, axis=-1, keepdims=True)
        h_ref[...] = (x * lax.rsqrt(ms + NORM_EPS) * g_ref[...]).astype(BF16)

    y = jnp.dot(h_ref[...], w_ref[...], preferred_element_type=F32)
    _store_with_rope(y, o_ref, j, tn, rope, tabs)


def normmm(z, xcol, g, w, *, ts, tn, rope=None, tabs=()):
    b, s, _ = z.shape
    k, n = w.shape
    tab_specs = [pl.BlockSpec((ts, LANES), lambda bi, i, j: (i, 0)) for _ in tabs]
    return pl.pallas_call(
        functools.partial(_normmm_kernel, tn=tn, rope=rope),
        grid=(b, s // ts, n // tn),
        in_specs=[
            pl.BlockSpec((1, ts, k), lambda bi, i, j: (bi, i, xcol)),
            pl.BlockSpec((1, k), lambda bi, i, j: (0, 0)),
            pl.BlockSpec((k, tn), lambda bi, i, j: (0, j)),
        ] + tab_specs,
        out_specs=pl.BlockSpec((1, ts, tn), lambda bi, i, j: (bi, i, j)),
        out_shape=jax.ShapeDtypeStruct((b, s, n), BF16),
        scratch_shapes=[pltpu.VMEM((ts, k), BF16)],
        compiler_params=_params("parallel", "parallel", "arbitrary"),
        name="normmm",
    )(z, g, w, *tabs)


LOG2E = math.log2(math.e)


def _flash(q, k_ref, v_ref, m_sc, l_sc, acc_sc, tk):
    seq = k_ref.shape[0]
    m_sc[...] = jnp.full(m_sc.shape, -jnp.inf, F32)
    l_sc[...] = jnp.zeros(l_sc.shape, F32)
    acc_sc[...] = jnp.zeros(acc_sc.shape, F32)
    nck = tk // LANES
    nchunk = seq // tk

    def scores(c):
        return lax.dot_general(q, k_ref[c * tk:(c + 1) * tk, :], (((1,), (1,)), ((), ())),
                               preferred_element_type=F32)

    s_next = scores(0)
    for c in range(nchunk):
        s = s_next
        if c + 1 < nchunk:
            s_next = scores(c + 1)
        sc = [s[:, i * LANES:(i + 1) * LANES] for i in range(nck)]
        m_part = functools.reduce(jnp.maximum, sc)
        m_prev = m_sc[...]
        m_new = jnp.maximum(m_prev, jnp.max(m_part, axis=-1, keepdims=True))
        alpha = jnp.exp2(m_prev - m_new)
        pc = [jnp.exp2(x - m_new) for x in sc]
        l_sc[...] = alpha * l_sc[...] + functools.reduce(jnp.add, pc)
        p = jnp.concatenate([x.astype(BF16) for x in pc], axis=1)
        acc_sc[...] = alpha * acc_sc[...] + jnp.dot(
            p, v_ref[c * tk:(c + 1) * tk, :], preferred_element_type=F32)
        m_sc[...] = m_new


def _row_total(l):
    return jnp.sum(l, axis=-1, keepdims=True)


def _mla_attn_kernel(q_ref, kn_ref, v_ref, kpe_ref, o_ref, k_sc, m_sc, l_sc, acc_sc, *, tk, scale):
    @pl.when(pl.program_id(2) == 0)
    def _():
        k_sc[:, :LANES] = kn_ref[0]
        k_sc[:, LANES:] = kpe_ref[0]

    q = (q_ref[0].astype(F32) * (scale * LOG2E)).astype(BF16)
    _flash(q, k_sc, v_ref.at[0], m_sc, l_sc, acc_sc, tk)
    o_ref[0] = (acc_sc[...] / _row_total(l_sc[...])).astype(o_ref.dtype)


def mla_attention(q, kv, z, kpe_col, *, tq, tk):
    b, s, _ = q.shape
    scale = (MLA_NOPE + MLA_ROPE) ** -0.5
    return pl.pallas_call(
        functools.partial(_mla_attn_kernel, tk=tk, scale=scale),
        grid=(b, MLA_HEADS, s // tq),
        in_specs=[
            pl.BlockSpec((1, tq, MLA_QK_PAD), lambda bi, h, i: (bi, i, h)),
            pl.BlockSpec((1, s, LANES), lambda bi, h, i: (bi, 0, 2 * h)),
            pl.BlockSpec((1, s, LANES), lambda bi, h, i: (bi, 0, 2 * h + 1)),
            pl.BlockSpec((1, s, LANES), lambda bi, h, i: (bi, 0, kpe_col)),
        ],
        out_specs=pl.BlockSpec((1, tq, MLA_V), lambda bi, h, i: (bi, i, h)),
        out_shape=jax.ShapeDtypeStruct((b, s, MLA_HEADS * MLA_V), BF16),
        scratch_shapes=[
            pltpu.VMEM((s, MLA_QK_PAD), BF16),
            pltpu.VMEM((tq, LANES), F32),
            pltpu.VMEM((tq, LANES), F32),
            pltpu.VMEM((tq, MLA_V), F32),
        ],
        compiler_params=_params("parallel", "parallel", "arbitrary"),
        name="mla_attention",
    )(q, kv, kv, z)


def _diff_attn_kernel(q_ref, k_ref, v_ref, lam_ref, g_ref, o_ref, m_sc, l_sc, acc_sc,
                      *, tq, tk, scale, lam_init):
    q = q_ref[0].astype(F32) * (scale * LOG2E)
    lane = lax.broadcasted_iota(jnp.int32, q.shape, 1)
    q1 = jnp.where(lane < DIFF_HD, q, 0.0).astype(BF16)
    q2 = jnp.where(lane >= DIFF_HD, q, 0.0).astype(BF16)
    _flash(jnp.concatenate([q1, q2], axis=0), k_ref.at[0], v_ref.at[0], m_sc, l_sc, acc_sc, tk)
    lam = lam_ref[...]
    lam_full = (jnp.exp(jnp.sum(lam[0:1] * lam[1:2], axis=-1, keepdims=True))
                - jnp.exp(jnp.sum(lam[2:3] * lam[3:4], axis=-1, keepdims=True)) + lam_init)
    o1 = acc_sc[:tq, :] / _row_total(l_sc[:tq, :])
    o2 = acc_sc[tq:, :] / _row_total(l_sc[tq:, :])
    o = o1 - lam_full * o2
    ms = jnp.mean(o * o, axis=-1, keepdims=True)
    o = o * lax.rsqrt(ms + DIFF_SUBLN_EPS) * g_ref[...] * (1.0 - lam_init)
    o_ref[0] = o.astype(o_ref.dtype)


def diff_attention(z, lam, subln_g, lam_init, *, tq, tk):
    b, s, _ = z.shape
    scale = DIFF_HD ** -0.5
    nh = DIFF_HEADS
    return pl.pallas_call(
        functools.partial(_diff_attn_kernel, tq=tq, tk=tk, scale=scale, lam_init=lam_init),
        grid=(b, nh, s // tq),
        in_specs=[
            pl.BlockSpec((1, tq, LANES), lambda bi, h, i: (bi, i, h)),
            pl.BlockSpec((1, s, LANES), lambda bi, h, i: (bi, 0, nh + h)),
            pl.BlockSpec((1, s, LANES), lambda bi, h, i: (bi, 0, 2 * nh + h)),
            pl.BlockSpec((4, DIFF_HD), lambda bi, h, i: (0, 0)),
            pl.BlockSpec((1, LANES), lambda bi, h, i: (0, 0)),
        ],
        out_specs=pl.BlockSpec((1, tq, LANES), lambda bi, h, i: (bi, i, h)),
        out_shape=jax.ShapeDtypeStruct((b, s, nh * LANES), BF16),
        scratch_shapes=[
            pltpu.VMEM((2 * tq, LANES), F32),
            pltpu.VMEM((2 * tq, LANES), F32),
            pltpu.VMEM((2 * tq, LANES), F32),
        ],
        compiler_params=_params("parallel", "parallel", "arbitrary"),
        name="diff_attention",
    )(z, z, z, lam, subln_g.reshape(1, LANES))


def _finish(a, w_ref, x_ref, gm_ref, gp_ref, o_ref):
    y = jnp.dot(a.astype(BF16), w_ref[...], preferred_element_type=F32)
    ms = jnp.mean(y * y, axis=-1, keepdims=True)
    o_ref[0] = x_ref[0] + gm_ref[0] * (y * lax.rsqrt(ms + NORM_EPS) * gp_ref[...])


def _outmm_kernel(a_ref, g_ref, w_ref, x_ref, gm_ref, gp_ref, o_ref):
    a = a_ref[0].astype(F32) * _silu(g_ref[0].astype(F32))
    _finish(a, w_ref, x_ref, gm_ref, gp_ref, o_ref)


def _outmm2_kernel(a1_ref, a2_ref, g_ref, w_ref, x_ref, gm_ref, gp_ref, o_ref):
    a = (a1_ref[0] + a2_ref[0]) * _silu(g_ref[0].astype(F32))
    _finish(a, w_ref, x_ref, gm_ref, gp_ref, o_ref)


def outmm(acts, zgate, gcol, w, x, gate_mod, g_post, *, ts):
    b, s, d = x.shape
    row = lambda bi, i: (bi, i, 0)
    kern = _outmm_kernel if len(acts) == 1 else _outmm2_kernel
    return pl.pallas_call(
        kern,
        grid=(b, s // ts),
        in_specs=[pl.BlockSpec((1, ts, d), row) for _ in acts] + [
            pl.BlockSpec((1, ts, d), lambda bi, i: (bi, i, gcol)),
            pl.BlockSpec((d, d), lambda bi, i: (0, 0)),
            pl.BlockSpec((1, ts, d), row),
            pl.BlockSpec((1, 1, d), lambda bi, i: (bi, 0, 0)),
            pl.BlockSpec((1, d), lambda bi, i: (0, 0)),
        ],
        out_specs=pl.BlockSpec((1, ts, d), row),
        out_shape=jax.ShapeDtypeStruct((b, s, d), F32),
        compiler_params=_params("parallel", "parallel"),
        name="outmm",
    )(*acts, zgate, w, x, gate_mod, g_post)


def _lru_prepare(xe_sc, x_ref, prev_ref, next_ref, is_first, is_last, cw_ref, cb_ref, gw_ref,
                 gb_ref, sp_ref, a_sc, u_sc, d, ts, edge_row):
    w = LRU_WIDTH
    h = LRU_HALO
    xe_sc[pl.ds(h, ts), :] = x_ref[0].astype(F32)

    @pl.when(is_first)
    def _():
        xe_sc[pl.ds(0, h), :] = jnp.zeros((h, w), F32)

    @pl.when(jnp.logical_not(is_first))
    def _():
        xe_sc[pl.ds(0, h), :] = prev_ref[0].astype(F32)

    @pl.when(is_last)
    def _():
        xe_sc[pl.ds(h + ts, h), :] = jnp.zeros((h, w), F32)

    @pl.when(jnp.logical_not(is_last))
    def _():
        xe_sc[pl.ds(h + ts, h), :] = next_ref[0].astype(F32)

    xc = cb_ref[...] + sum(cw_ref[i:i + 1, :] * xe_sc[pl.ds(h - 1 + i, ts), :] for i in range(4))
    row = lax.broadcasted_iota(jnp.int32, (ts, LRU_BW), 0)
    for n in range(LRU_BLOCKS):
        cols = slice(n * LRU_BW, (n + 1) * LRU_BW)
        xn = xc[:, cols]
        g = jnp.dot(xn.astype(BF16), gw_ref[d, n], preferred_element_type=F32)
        r_t = jax.nn.sigmoid(g[:, :LRU_BW] + gb_ref[2 * d:2 * d + 1, cols])
        i_t = jax.nn.sigmoid(g[:, LRU_BW:] + gb_ref[2 * d + 1:2 * d + 2, cols])
        log_a = -LRU_C * r_t * sp_ref[d:d + 1, cols]
        a_t = jnp.exp(log_a)
        mult = jnp.sqrt(-jnp.tanh(log_a) * (1.0 + a_t * a_t))
        mult = jnp.where(row == edge_row, 1.0, mult)
        a_sc[:, cols] = a_t
        u_sc[:, cols] = mult * i_t * xn


def _lru_kernel(xf_ref, pf_ref, nf_ref, xb_ref, pb_ref, nb_ref, cw_ref, cb_ref, gw_ref, gb_ref,
                lam_ref, yf_ref, yb_ref, xe_sc, af_sc, uf_sc, ab_sc, ub_sc, hf_sc, hb_sc, sp_sc,
                *, ts):
    i = pl.program_id(1)
    nt = pl.num_programs(1)

    @pl.when(i == 0)
    def _():
        hf_sc[...] = jnp.zeros(hf_sc.shape, F32)
        hb_sc[...] = jnp.zeros(hb_sc.shape, F32)
        nl = -lam_ref[...]
        sp_sc[...] = jnp.maximum(nl, 0.0) + jnp.log(1.0 + jnp.exp(-jnp.abs(nl)))

    first, last = i == 0, i == nt - 1
    _lru_prepare(xe_sc, xf_ref, pf_ref, nf_ref, first, last, cw_ref, cb_ref, gw_ref, gb_ref,
                 sp_sc, af_sc, uf_sc, 0, ts, jnp.where(first, 0, -1))
    _lru_prepare(xe_sc, xb_ref, pb_ref, nb_ref, last, first, cw_ref, cb_ref, gw_ref, gb_ref,
                 sp_sc, ab_sc, ub_sc, 1, ts, jnp.where(first, ts - 1, -1))

    def step(k, carry):
        hf, hb = carry
        kb = ts - 1 - k
        hf = af_sc[pl.ds(k, 1), :] * hf + uf_sc[pl.ds(k, 1), :]
        hb = ab_sc[pl.ds(kb, 1), :] * hb + ub_sc[pl.ds(kb, 1), :]
        yf_ref[0, pl.ds(k, 1), :] = hf
        yb_ref[0, pl.ds(kb, 1), :] = hb
        return hf, hb

    hf, hb = lax.fori_loop(0, ts, step, (hf_sc[...], hb_sc[...]), unroll=8)
    hf_sc[...] = hf
    hb_sc[...] = hb


def lru_scan(z, conv_w, conv_b, gate_w, gate_b, lam, *, ts):
    b, s, _ = z.shape
    w = LRU_WIDTH
    h = LRU_HALO
    nt = s // ts
    rpb = ts // h
    nhb = s // h
    fwd = lambda bi, i: (bi, i, 0)
    bwd = lambda bi, i: (bi, nt - 1 - i, 0)
    fwd_prev = lambda bi, i: (bi, jnp.maximum(i * rpb - 1, 0), 0)
    fwd_next = lambda bi, i: (bi, jnp.minimum((i + 1) * rpb, nhb - 1), 0)
    bwd_prev = lambda bi, i: (bi, jnp.maximum((nt - 1 - i) * rpb - 1, 0), 0)
    bwd_next = lambda bi, i: (bi, jnp.minimum((nt - i) * rpb, nhb - 1), 0)
    full = lambda *shape: pl.BlockSpec(shape, lambda bi, i: (0,) * len(shape))
    return pl.pallas_call(
        functools.partial(_lru_kernel, ts=ts),
        grid=(b, nt),
        in_specs=[
            pl.BlockSpec((1, ts, w), fwd), pl.BlockSpec((1, h, w), fwd_prev),
            pl.BlockSpec((1, h, w), fwd_next),
            pl.BlockSpec((1, ts, w), bwd), pl.BlockSpec((1, h, w), bwd_prev),
            pl.BlockSpec((1, h, w), bwd_next),
            full(4, w), full(1, w), full(2, LRU_BLOCKS, LRU_BW, 2 * LRU_BW), full(4, w), full(2, w),
        ],
        out_specs=[pl.BlockSpec((1, ts, w), fwd), pl.BlockSpec((1, ts, w), bwd)],
        out_shape=[jax.ShapeDtypeStruct((b, s, w), F32)] * 2,
        scratch_shapes=[
            pltpu.VMEM((ts + 2 * h, w), F32),
            pltpu.VMEM((ts, w), F32), pltpu.VMEM((ts, w), F32),
            pltpu.VMEM((ts, w), F32), pltpu.VMEM((ts, w), F32),
            pltpu.VMEM((1, w), F32), pltpu.VMEM((1, w), F32), pltpu.VMEM((2, w), F32),
        ],
        compiler_params=_params("parallel", "arbitrary"),
        name="lru_scan",
    )(z, z, z, z, z, z, conv_w, conv_b, gate_w, gate_b, lam)


RWKV_HALO = 8


def _rwkv_shifted(x_ref, p_ref, n_ref, g_ref, sc_ref, sh_ref, he_sc, is_first, is_last, ts):
    hl = RWKV_HALO
    d = D_MODEL
    g, sc, sh = g_ref[...], sc_ref[0], sh_ref[0]
    he_sc[pl.ds(hl, ts), :] = _modnorm(x_ref[0], g, sc, sh)

    @pl.when(is_first)
    def _():
        he_sc[pl.ds(0, hl), :] = jnp.zeros((hl, d), F32)

    @pl.when(jnp.logical_not(is_first))
    def _():
        he_sc[pl.ds(0, hl), :] = _modnorm(p_ref[0], g, sc, sh)

    @pl.when(is_last)
    def _():
        he_sc[pl.ds(hl + ts, hl), :] = jnp.zeros((hl, d), F32)

    @pl.when(jnp.logical_not(is_last))
    def _():
        he_sc[pl.ds(hl + ts, hl), :] = _modnorm(n_ref[0], g, sc, sh)

    h = he_sc[pl.ds(hl, ts), :]
    xx = 0.5 * (he_sc[pl.ds(hl - 1, ts), :] + he_sc[pl.ds(hl + 1, ts), :]) - h
    return h, xx


def _rwkv_proj_kernel(x_ref, p_ref, n_ref, g_ref, sc_ref, sh_ref, mu_ref, w_ref, o_ref,
                      he_sc, xx_sc, *, ts):
    i, m = pl.program_id(1), pl.program_id(2)

    @pl.when(m == 0)
    def _():
        _, xx = _rwkv_shifted(x_ref, p_ref, n_ref, g_ref, sc_ref, sh_ref, he_sc,
                              i == 0, i == pl.num_programs(1) - 1, ts)
        xx_sc[...] = xx

    xs = he_sc[pl.ds(RWKV_HALO, ts), :] + xx_sc[...] * mu_ref[0]
    o_ref[0] = jnp.dot(xs.astype(BF16), w_ref[0], preferred_element_type=F32).astype(o_ref.dtype)


def _halo_specs(ts, s, halo, d):
    rpb, nhb = ts // halo, s // halo
    prev = lambda bi, i, *_: (bi, jnp.maximum(i * rpb - 1, 0), 0)
    nxt = lambda bi, i, *_: (bi, jnp.minimum((i + 1) * rpb, nhb - 1), 0)
    return pl.BlockSpec((1, halo, d), prev), pl.BlockSpec((1, halo, d), nxt)


def rwkv_proj(x, g, scale, shift, mu, w_in, *, ts):
    b, s, d = x.shape
    prev, nxt = _halo_specs(ts, s, RWKV_HALO, d)
    return pl.pallas_call(
        functools.partial(_rwkv_proj_kernel, ts=ts),
        grid=(b, s // ts, 4),
        in_specs=[
            pl.BlockSpec((1, ts, d), lambda bi, i, m: (bi, i, 0)), prev, nxt,
            pl.BlockSpec((1, d), lambda bi, i, m: (0, 0)),
            pl.BlockSpec((1, 1, d), lambda bi, i, m: (bi, 0, 0)),
            pl.BlockSpec((1, 1, d), lambda bi, i, m: (bi, 0, 0)),
            pl.BlockSpec((1, 1, d), lambda bi, i, m: (m, 0, 0)),
            pl.BlockSpec((1, d, d), lambda bi, i, m: (m, 0, 0)),
        ],
        out_specs=pl.BlockSpec((1, ts, d), lambda bi, i, m: (bi, i, m)),
        out_shape=jax.ShapeDtypeStruct((b, s, 4 * d), BF16),
        scratch_shapes=[pltpu.VMEM((ts + 2 * RWKV_HALO, d), F32), pltpu.VMEM((ts, d), F32)],
        compiler_params=_params("parallel", "parallel", "arbitrary"),
        name="rwkv_proj",
    )(x, x, x, g, scale, shift, mu.reshape(6, 1, d), w_in)


def _neg_softplus_neg(x):
    return -(jnp.maximum(-x, 0.0) + jnp.log(1.0 + jnp.exp(-jnp.abs(x))))


def _rwkv_lora_kernel(x_ref, p_ref, n_ref, g_ref, sc_ref, sh_ref, mu_ref, w0_ref, w1_ref, w2_ref,
                      a0_ref, a1_ref, a2_ref, lw_ref, a_ref, he_sc, *, ts):
    i = pl.program_id(1)
    h, xx = _rwkv_shifted(x_ref, p_ref, n_ref, g_ref, sc_ref, sh_ref, he_sc,
                          i == 0, i == pl.num_programs(1) - 1, ts)
    xw = (h + xx * mu_ref[4]).astype(BF16)
    xa = (h + xx * mu_ref[5]).astype(BF16)
    for dr in range(2):
        t1 = jnp.tanh(jnp.dot(xw, w1_ref[dr], preferred_element_type=F32))
        wl = w0_ref[dr] + jnp.dot(t1.astype(BF16), w2_ref[dr], preferred_element_type=F32)
        lw_ref[dr, 0] = -jnp.exp(_neg_softplus_neg(wl) - 0.5)
        t2 = jnp.dot(xa, a1_ref[dr], preferred_element_type=F32)
        al = a0_ref[dr] + jnp.dot(t2.astype(BF16), a2_ref[dr], preferred_element_type=F32)
        a_ref[dr, 0] = jax.nn.sigmoid(al).astype(a_ref.dtype)


def rwkv_lora(x, g, scale, shift, mu, w0, w1, w2, a0, a1, a2, *, ts):
    b, s, d = x.shape
    lp = RWKV_LORA_PAD
    prev, nxt = _halo_specs(ts, s, RWKV_HALO, d)
    full = lambda *shape: pl.BlockSpec(shape, lambda bi, i: (0,) * len(shape))
    out = pl.BlockSpec((2, 1, ts, d), lambda bi, i: (0, bi, i, 0))
    return pl.pallas_call(
        functools.partial(_rwkv_lora_kernel, ts=ts),
        grid=(b, s // ts),
        in_specs=[
            pl.BlockSpec((1, ts, d), lambda bi, i: (bi, i, 0)), prev, nxt,
            full(1, d),
            pl.BlockSpec((1, 1, d), lambda bi, i: (bi, 0, 0)),
            pl.BlockSpec((1, 1, d), lambda bi, i: (bi, 0, 0)),
            full(6, 1, d), full(2, 1, d), full(2, d, lp), full(2, lp, d),
            full(2, 1, d), full(2, d, lp), full(2, lp, d),
        ],
        out_specs=[out, out],
        out_shape=[jax.ShapeDtypeStruct((2, b, s, d), F32), jax.ShapeDtypeStruct((2, b, s, d), BF16)],
        scratch_shapes=[pltpu.VMEM((ts + 2 * RWKV_HALO, d), F32)],
        compiler_params=_params("parallel", "parallel"),
        name="rwkv_lora",
    )(x, x, x, g, scale, shift, mu.reshape(6, 1, d), w0.reshape(2, 1, d), w1, w2,
      a0.reshape(2, 1, d), a1, a2)


def _hilo(x):
    hi = x.astype(BF16)
    return hi, (x - hi.astype(F32)).astype(BF16)


def _dot2(x, w):
    hi, lo = _hilo(x)
    return (jnp.dot(hi, w, preferred_element_type=F32) + jnp.dot(lo, w, preferred_element_type=F32))


def _nt(x, y):
    return lax.dot_general(x, y, (((1,), (1,)), ((), ())), preferred_element_type=F32)


def _bdot(x, y):
    return jnp.dot(x.astype(BF16), y.astype(BF16), preferred_element_type=F32)


def _rwkv_masks(reverse):
    t = RWKV_CHUNK
    n = RWKV_N
    lane_t = lax.broadcasted_iota(jnp.int32, (t, LANES), 1)
    row_t = lax.broadcasted_iota(jnp.int32, (t, LANES), 0)
    col_t = lane_t % n
    r128 = lax.broadcasted_iota(jnp.int32, (LANES, LANES), 0)
    c128 = lax.broadcasted_iota(jnp.int32, (LANES, LANES), 1)
    rt = lax.broadcasted_iota(jnp.int32, (t, t), 0)
    ct = lax.broadcasted_iota(jnp.int32, (t, t), 1)
    if reverse:
        strict2, incl2, tri = col_t > row_t, col_t >= row_t, ct >= rt
    else:
        strict2, incl2, tri = col_t < row_t, col_t <= row_t, ct <= rt
    same_head = (r128 // n) == (c128 // n)
    levels = []
    sz = 1
    while sz < t:
        hi_r, lo_c = (r128 % (2 * sz)) >= sz, (c128 % (2 * sz)) < sz
        if reverse:
            off = jnp.logical_and(jnp.logical_not(hi_r), jnp.logical_not(lo_c))
        else:
            off = jnp.logical_and(hi_r, lo_c)
        levels.append(jnp.logical_and(off, (r128 // (2 * sz)) == (c128 // (2 * sz))))
        sz *= 2
    return dict(m1=lane_t < n, strict2=strict2, incl2=incl2,
                tri=jnp.where(tri, 1.0, 0.0).astype(BF16), same_head=same_head,
                ones_bd=jnp.where(same_head, 1.0, 0.0).astype(BF16),
                eye=jnp.where(r128 == c128, 1.0, 0.0), levels=levels)


def _each(f, *lists):
    return [f(*xs) for xs in zip(*lists)]


def _rwkv_chunk_terms(lw, r, k, v, a, kk_scale, ka_scale, rk_scale, mk, reverse):
    t = RWKV_CHUNK
    m1, strict2, incl2, tri, same_head = mk['m1'], mk['strict2'], mk['incl2'], mk['tri'], mk['same_head']

    def sel(top, bot):
        return jnp.where(m1, top, bot)

    def stack(x):
        return jnp.concatenate([jnp.where(m1, x, 0.0), jnp.where(m1, 0.0, x)], axis=0)

    kk = _each(jnp.multiply, k, kk_scale)
    kd = _each(lambda k_, a_, s_: k_ * (1.0 + (a_ - 1.0) * s_), k, a, ka_scale)
    seg = _each(lambda kk_, r_, kd_, s_: _bdot(
        jnp.concatenate([kk_ * kk_, r_ * kd_ * s_], axis=0), mk['ones_bd']), kk, r, kd, rk_scale)
    kk = _each(lambda kk_, sg: kk_ / jnp.maximum(jnp.sqrt(sg[:t]), 1e-12), kk, seg)
    bonus = _each(lambda sg, v_: sg[t:] * v_, seg, v)

    c = _each(lambda lw_: _dot2_left(tri, lw_), lw)
    e_pos = _each(jnp.exp, c)
    e_neg = _each(lambda c_: jnp.exp(-c_), c)
    at = _each(lambda kk_, c_, lw_: -kk_ * jnp.exp(c_ - lw_), kk, c, lw)
    rt_ = _each(jnp.multiply, r, e_pos)
    bt = _each(lambda kk_, a_, e_: kk_ * a_ * e_, kk, a, e_neg)
    kt = _each(jnp.multiply, kd, e_neg)
    big = _each(lambda at_, rt__, bt_, kt_: _nt(
        jnp.concatenate([at_, rt__], axis=0).astype(BF16),
        jnp.concatenate([stack(bt_), stack(kt_)], axis=0).astype(BF16)), at, rt_, bt, kt)
    aak = _each(lambda b_: jnp.where(strict2, b_[:t, LANES:], 0.0), big)
    rb = _each(lambda b_: jnp.where(incl2, b_[t:, :LANES], 0.0), big)
    rk = _each(lambda b_: jnp.where(incl2, b_[t:, LANES:], 0.0), big)

    abd = _each(lambda b_: stack(jnp.where(strict2, b_[:t, :LANES], 0.0)), big)
    dinv = _each(lambda ab: mk['eye'] + jnp.where(mk['levels'][0], ab, 0.0), abd)
    for off in mk['levels'][1:]:
        dinv_b = _each(lambda d_: d_.astype(BF16), dinv)
        half = _each(lambda db, ab: jnp.dot(db, jnp.where(off, ab, 0.0).astype(BF16),
                                            preferred_element_type=F32), dinv_b, abd)
        dinv = _each(lambda d_, h_, db: d_ + jnp.dot(h_.astype(BF16), db, preferred_element_type=F32),
                     dinv, half, dinv_b)

    vst = _each(stack, v)
    aakv = _each(_bdot, aak, vst)
    x = _each(lambda d_, at_, av: _bdot(d_, jnp.concatenate(
        [jnp.concatenate([at_, av], axis=1)] * 2, axis=0)), dinv, at, aakv)
    atp = _each(lambda x_: sel(x_[:t, :LANES], x_[t:, :LANES]), x)
    u0 = _each(lambda x_: sel(x_[:t, LANES:], x_[t:, LANES:]), x)
    y = _each(lambda rb_, p_, u_: _bdot(rb_, jnp.concatenate([stack(p_), stack(u_)], axis=1)),
              rb, atp, u0)
    rkv = _each(_bdot, rk, vst)
    rp = _each(lambda rt__, y_: rt__ + y_[:, :LANES], rt_, y)
    o0 = _each(lambda y_, q_: y_[:, LANES:] + q_, y, rkv)
    gmt = _each(lambda p_, bt_: jnp.where(same_head, _bdot(p_.T, bt_), 0.0), atp, bt)
    zty = _each(lambda u_, v_, bt_, kt_: jnp.where(same_head, _bdot(
        jnp.concatenate([u_, v_], axis=0).T, jnp.concatenate([bt_, kt_], axis=0)), 0.0),
        u0, v, bt, kt)
    decay_row = _each(lambda e_: e_[0:1, :] if reverse else e_[t - 1:t, :], e_pos)
    return list(zip(rp, o0, gmt, zty, decay_row, bonus))


def _dot2_left(w, x):
    hi, lo = _hilo(x)
    return (jnp.dot(w, hi, preferred_element_type=F32) + jnp.dot(w, lo, preferred_element_type=F32))


def _rwkv_scan_kernel(lw_ref, a_ref, r_ref, k_ref, v_ref, kk_ref, ka_ref, rk_ref, o_ref, bo_ref,
                      s_sc, *, ts, npairs, reverse):
    @pl.when(pl.program_id(2) == 0)
    def _():
        s_sc[...] = jnp.zeros(s_sc.shape, F32)

    t = RWKV_CHUNK
    nch = ts // t
    order = range(nch - 1, -1, -1) if reverse else range(nch)
    mk = _rwkv_masks(reverse)
    chains = [(ci, hp) for ci in order for hp in range(npairs)]
    rows = lambda ci: slice(ci * t, (ci + 1) * t)
    cols = lambda hp: slice(hp * LANES, (hp + 1) * LANES)
    blk = lambda ref: [ref[0, rows(ci), cols(hp)].astype(F32) for ci, hp in chains]
    blk4 = lambda ref: [ref[0, 0, rows(ci), cols(hp)].astype(F32) for ci, hp in chains]
    par = lambda ref: [ref[:, cols(hp)] for _, hp in chains]
    terms = dict(zip(chains, _rwkv_chunk_terms(
        blk4(lw_ref), blk(r_ref), blk(k_ref), blk(v_ref), blk4(a_ref),
        par(kk_ref), par(ka_ref), par(rk_ref), mk, reverse)))
    s = [s_sc[hp] for hp in range(npairs)]
    for ci in order:
        for hp in range(npairs):
            rp, o0, gmt, zty, decay_row, bonus = terms[ci, hp]
            s_b = s[hp].astype(BF16)
            o_ref[0, rows(ci), cols(hp)] = o0 + _nt(rp.astype(BF16), s_b)
            bo_ref[0, rows(ci), cols(hp)] = bonus.astype(bo_ref.dtype)
            s[hp] = (s[hp] + jnp.dot(s_b, gmt.astype(BF16), preferred_element_type=F32)
                     + zty) * decay_row
    for hp in range(npairs):
        s_sc[hp] = s[hp]


def rwkv_scan(lw, a, rkvg, k_k, k_a, r_k, dr, *, ts, npairs):
    _, b, s, d = lw.shape
    nt = s // ts
    wd = npairs * LANES
    ng = d // wd
    reverse = dr == 1
    tmap = (lambda i: nt - 1 - i) if reverse else (lambda i: i)
    dspec = pl.BlockSpec((1, 1, ts, wd), lambda bi, g, i: (dr, bi, tmap(i), g))
    col = lambda m: pl.BlockSpec((1, ts, wd), lambda bi, g, i: (bi, tmap(i), m * ng + g))
    par = pl.BlockSpec((1, wd), lambda bi, g, i: (0, g))
    out = pl.BlockSpec((1, ts, wd), lambda bi, g, i: (bi, tmap(i), g))
    return pl.pallas_call(
        functools.partial(_rwkv_scan_kernel, ts=ts, npairs=npairs, reverse=reverse),
        grid=(b, ng, nt),
        in_specs=[dspec, dspec, col(0), col(1), col(2), par, par, par],
        out_specs=[out, out],
        out_shape=[jax.ShapeDtypeStruct((b, s, d), F32), jax.ShapeDtypeStruct((b, s, d), BF16)],
        scratch_shapes=[pltpu.VMEM((npairs, LANES, LANES), F32)],
        compiler_params=_params("parallel", "parallel", "arbitrary"),
        name="rwkv_scan_bwd" if reverse else "rwkv_scan_fwd",
    )(lw, a, rkvg, rkvg, rkvg, k_k, k_a, r_k)


def _rwkv_out_kernel(of_ref, ob_ref, bf_ref, bb_ref, g_ref, lg_ref, lb_ref, w_ref, x_ref, gm_ref,
                     gp_ref, o_ref, a_sc):
    r128 = lax.broadcasted_iota(jnp.int32, (LANES, LANES), 0)
    c128 = lax.broadcasted_iota(jnp.int32, (LANES, LANES), 1)
    avg = jnp.where((r128 // RWKV_N) == (c128 // RWKV_N), 1.0 / RWKV_N, 0.0).astype(BF16)
    for c in range(D_MODEL // LANES):
        cols = slice(c * LANES, (c + 1) * LANES)
        o = of_ref[0, :, cols] + ob_ref[0, :, cols]
        cen = o - _dot2(o, avg)
        var = _dot2(cen * cen, avg)
        on = cen * lax.rsqrt(var + RWKV_GN_EPS)
        y = (on * lg_ref[:, cols] + lb_ref[:, cols]
             + bf_ref[0, :, cols].astype(F32) + bb_ref[0, :, cols].astype(F32))
        a_sc[:, cols] = (y * _silu(g_ref[0, :, cols].astype(F32))).astype(BF16)
    _finish(a_sc[...], w_ref, x_ref, gm_ref, gp_ref, o_ref)


def rwkv_out(o_f, o_b, bo_f, bo_b, rkvg, ln_g, ln_b, w, x, gate_mod, g_post, *, ts):
    b, s, d = x.shape
    row = lambda bi, i: (bi, i, 0)
    vec = pl.BlockSpec((1, d), lambda bi, i: (0, 0))
    return pl.pallas_call(
        _rwkv_out_kernel,
        grid=(b, s // ts),
        in_specs=[pl.BlockSpec((1, ts, d), row)] * 4 + [
            pl.BlockSpec((1, ts, d), lambda bi, i: (bi, i, 3)), vec, vec,
            pl.BlockSpec((d, d), lambda bi, i: (0, 0)),
            pl.BlockSpec((1, ts, d), row),
            pl.BlockSpec((1, 1, d), lambda bi, i: (bi, 0, 0)),
            vec,
        ],
        out_specs=pl.BlockSpec((1, ts, d), row),
        out_shape=jax.ShapeDtypeStruct((b, s, d), F32),
        scratch_shapes=[pltpu.VMEM((ts, d), BF16)],
        compiler_params=_params("parallel", "parallel"),
        name="rwkv_out",
    )(o_f, o_b, bo_f, bo_b, rkvg, ln_g, ln_b, w, x, gate_mod, g_post)


def _prep_weights(p):
    d = D_MODEL
    w = {}
    wi = p['mla_w_in'][0]
    q_lat, kv_lat, k_pe, gate = jnp.split(
        wi, [MLA_Q_RANK, MLA_Q_RANK + MLA_KV_RANK, MLA_Q_RANK + MLA_KV_RANK + MLA_ROPE], axis=1)
    w['mla_in'] = jnp.concatenate(
        [gate, q_lat, kv_lat, k_pe, jnp.zeros((d, LANES - MLA_ROPE), F32)], axis=1).astype(BF16)
    wq = p['mla_w_q_up'][0].reshape(MLA_Q_RANK, MLA_HEADS, MLA_NOPE + MLA_ROPE)
    wq = jnp.pad(wq, ((0, 0), (0, 0), (0, MLA_QK_PAD - MLA_NOPE - MLA_ROPE)))
    w['mla_q'] = wq.reshape(MLA_Q_RANK, MLA_HEADS * MLA_QK_PAD).astype(BF16)
    w['mla_kv'] = p['mla_w_kv_up'][0].astype(BF16)
    w['mla_out'] = p['mla_w_out'][0].astype(BF16)
    w['diff_in'] = p['diff_w_in'][0].astype(BF16)
    w['diff_out'] = p['diff_w_out'][0].astype(BF16)
    w['lru_in'] = p['lru_w_in'][0].astype(BF16)
    w['lru_out'] = p['lru_w_out'][0].astype(BF16)
    gw = p['lru_gate_w'][0]
    w['lru_gate'] = jnp.concatenate([gw[:, 0], gw[:, 1]], axis=-1).astype(BF16)
    w['rwkv_in'] = p['rwkv_w_in'][0].astype(BF16)
    pad = RWKV_LORA_PAD - RWKV_LORA
    for nm in ('w1', 'a1'):
        w['rwkv_' + nm] = jnp.pad(p['rwkv_' + nm][0], ((0, 0), (0, 0), (0, pad))).astype(BF16)
    for nm in ('w2', 'a2'):
        w['rwkv_' + nm] = jnp.pad(p['rwkv_' + nm][0], ((0, 0), (0, pad), (0, 0))).astype(BF16)
    w['rwkv_out'] = p['rwkv_w_out'][0].astype(BF16)
    return w


def _mla_layer(x, pre_g, scale, shift, gate_mod, post_g, p, w, tabs):
    mla_tabs = tabs['mla']
    z = modmm(x, pre_g, scale, shift, w['mla_in'], ts=1024, tn=640,
              rope=(24, 25, MLA_ROPE // 2, 1, 0), tabs=mla_tabs)
    q = normmm(z, 4, p['mla_q_norm_g'], w['mla_q'], ts=1024, tn=512,
               rope=(0, 2 * MLA_HEADS, MLA_ROPE // 2, 2, 1), tabs=mla_tabs)
    kv = normmm(z, 5, p['mla_kv_norm_g'], w['mla_kv'], ts=1024, tn=512)
    o = mla_attention(q, kv, z, 24, tq=1024, tk=1024)
    return outmm([o], z, 0, w['mla_out'], x, gate_mod, post_g, ts=256)


def _diff_layer(x, layer_idx, pre_g, scale, shift, gate_mod, post_g, p, w, tabs):
    z = modmm(x, pre_g, scale, shift, w['diff_in'], ts=1024, tn=512,
              rope=(0, 2 * DIFF_HEADS, DIFF_ROT // 2, 1, 0), tabs=tabs['diff'])
    lam_init = 0.8 - 0.6 * math.exp(-0.3 * layer_idx)
    o = diff_attention(z, p['diff_lambda'][0], p['diff_subln_g'][0], lam_init, tq=512, tk=1024)
    return outmm([o], z, 3, w['diff_out'], x, gate_mod, post_g, ts=256)


def _lru_layer(x, pre_g, scale, shift, gate_mod, post_g, p, w):
    z = modmm(x, pre_g, scale, shift, w['lru_in'], ts=1024, tn=512)
    y_f, y_b = lru_scan(z, p['lru_conv_w'][0], p['lru_conv_b'], w['lru_gate'],
                        p['lru_gate_b'][0].reshape(4, LRU_WIDTH), p['lru_lambda'][0], ts=256)
    return outmm([y_f, y_b], z, 1, w['lru_out'], x, gate_mod, post_g, ts=256)


def _rwkv_layer(x, pre_g, scale, shift, gate_mod, post_g, p, w):
    mu = p['rwkv_mu'][0]
    rkvg = rwkv_proj(x, pre_g, scale, shift, mu, w['rwkv_in'], ts=512)
    lw, a = rwkv_lora(x, pre_g, scale, shift, mu, p['rwkv_w0'][0], w['rwkv_w1'], w['rwkv_w2'],
                      p['rwkv_a0'][0], w['rwkv_a1'], w['rwkv_a2'], ts=256)
    k_k = p['rwkv_k_k']
    k_a = p['rwkv_k_a']
    r_k = p['rwkv_r_k'].reshape(1, D_MODEL)
    o_f, bo_f = rwkv_scan(lw, a, rkvg, k_k, k_a, r_k, 0, ts=256, npairs=4)
    o_b, bo_b = rwkv_scan(lw, a, rkvg, k_k, k_a, r_k, 1, ts=256, npairs=4)
    return rwkv_out(o_f, o_b, bo_f, bo_b, rkvg, p['rwkv_ln_g'], p['rwkv_ln_b'], w['rwkv_out'],
                    x, gate_mod, post_g, ts=256)


def _trunk(x, mods, p, w):
    b, s, d = x.shape
    tabs = {
        'mla': _rope_tables(s, 1, LANES, MLA_ROPE, MLA_THETA, 0.0),
        'diff': _rope_tables(s, LANES // DIFF_HD, DIFF_HD, DIFF_ROT, ROPE_THETA, 1.0),
    }
    for i in range(DEPTH):
        shift, scale, gate_mod = (mods[i, :, k * d:(k + 1) * d].reshape(b, 1, d) for k in range(3))
        pre_g = p['norm_pre_g'][i:i + 1]
        post_g = p['norm_post_g'][i:i + 1]
        args = (pre_g, scale, shift, gate_mod, post_g, p, w)
        if i % 4 == 0:
            x = _mla_layer(x, *args, tabs)
        elif i % 4 == 1:
            x = _diff_layer(x, i, *args, tabs)
        elif i % 4 == 2:
            x = _lru_layer(x, *args)
        else:
            x = _rwkv_layer(x, *args)
    return x


def kernel(x_prompt, x_sample, c_prompt, c_sample, ada_w, ada_b, norm_pre_g, norm_post_g, mla_w_in, mla_q_norm_g, mla_kv_norm_g, mla_w_q_up, mla_w_kv_up, mla_w_out, diff_w_in, diff_lambda, diff_subln_g, diff_w_out, lru_w_in, lru_conv_w, lru_conv_b, lru_gate_w, lru_gate_b, lru_lambda, lru_w_out, rwkv_mu, rwkv_w_in, rwkv_w0, rwkv_w1, rwkv_w2, rwkv_a0, rwkv_a1, rwkv_a2, rwkv_k_k, rwkv_k_a, rwkv_r_k, rwkv_ln_g, rwkv_ln_b, rwkv_w_out):
    p = dict(
        norm_pre_g=norm_pre_g, norm_post_g=norm_post_g,
        mla_w_in=mla_w_in, mla_q_norm_g=mla_q_norm_g, mla_kv_norm_g=mla_kv_norm_g,
        mla_w_q_up=mla_w_q_up, mla_w_kv_up=mla_w_kv_up, mla_w_out=mla_w_out,
        diff_w_in=diff_w_in, diff_lambda=diff_lambda, diff_subln_g=diff_subln_g,
        diff_w_out=diff_w_out,
        lru_w_in=lru_w_in, lru_conv_w=lru_conv_w, lru_conv_b=lru_conv_b, lru_gate_w=lru_gate_w,
        lru_gate_b=lru_gate_b, lru_lambda=lru_lambda, lru_w_out=lru_w_out,
        rwkv_mu=rwkv_mu, rwkv_w_in=rwkv_w_in, rwkv_w0=rwkv_w0, rwkv_w1=rwkv_w1, rwkv_w2=rwkv_w2,
        rwkv_a0=rwkv_a0, rwkv_a1=rwkv_a1, rwkv_a2=rwkv_a2, rwkv_k_k=rwkv_k_k, rwkv_k_a=rwkv_k_a,
        rwkv_r_k=rwkv_r_k, rwkv_ln_g=rwkv_ln_g, rwkv_ln_b=rwkv_ln_b, rwkv_w_out=rwkv_w_out,
    )
    w = _prep_weights(p)
    nb = x_prompt.shape[0]
    mods = ada_mod(jnp.concatenate([c_prompt, c_sample], axis=0), ada_w, ada_b)
    y_prompt = _trunk(x_prompt, mods[:, :nb], p, w)
    y_sample = _trunk(x_sample, mods[:, nb:], p, w)
    return (y_prompt, y_sample)
```

```python
import functools
import math

import jax
import jax.numpy as jnp
from jax import lax
from jax.experimental import pallas as pl
from jax.experimental.pallas import tpu as pltpu

F32 = jnp.float32
BF16 = jnp.bfloat16

D_MODEL = 2048
DEPTH = 4
NORM_EPS = 1e-6
LANES = 128
VMEM_LIMIT = 56 * 1024 * 1024

MLA_HEADS = 16
MLA_Q_RANK = 512
MLA_KV_RANK = 512
MLA_NOPE = 128
MLA_ROPE = 64
MLA_V = 128
MLA_THETA = 10000.0
MLA_QK_PAD = 256

DIFF_HEADS = 16
DIFF_HD = 64
DIFF_ROT = 16
ROPE_THETA = 500000.0
DIFF_SUBLN_EPS = 1e-5

LRU_WIDTH = 2048
LRU_BLOCKS = 16
LRU_BW = 128
LRU_C = 8.0
LRU_HALO = 16

RWKV_N = 64
RWKV_HEADS = 32
RWKV_LORA = 96
RWKV_LORA_PAD = 128
RWKV_GN_EPS = 64e-5
RWKV_CHUNK = 64


def _params(*sem):
    return pltpu.CompilerParams(dimension_semantics=sem, vmem_limit_bytes=VMEM_LIMIT)


def _silu(x):
    return x * jax.nn.sigmoid(x)


def _rope128(y, c, s1, s2, shift):
    return y * c + pltpu.roll(y, LANES - shift, 1) * s1 + pltpu.roll(y, shift, 1) * s2


def _rope_tables(seq, heads_per_chunk, head_dim, rot_dim, theta, pad_cos):
    half = rot_dim // 2
    inv = 1.0 / (theta ** (jnp.arange(0, rot_dim, 2, dtype=F32) / rot_dim))
    ang = jnp.arange(seq, dtype=F32)[:, None] * inv[None, :]
    cos, sin = jnp.cos(ang), jnp.sin(ang)
    rest = head_dim - rot_dim
    c = jnp.concatenate([cos, cos, jnp.full((seq, rest), pad_cos, F32)], axis=-1)
    s1 = jnp.concatenate([-sin, jnp.zeros((seq, half + rest), F32)], axis=-1)
    s2 = jnp.concatenate([jnp.zeros((seq, half), F32), sin, jnp.zeros((seq, rest), F32)], axis=-1)
    tile = lambda t: jnp.tile(t, (1, heads_per_chunk))
    return tile(c), tile(s1), tile(s2)


def _ada_kernel(c_ref, w_ref, b_ref, o_ref):
    cs = _silu(c_ref[...]).astype(BF16)
    o_ref[0] = jnp.dot(cs, w_ref[0].astype(BF16), preferred_element_type=F32) + b_ref[0]


def ada_mod(c, ada_w, ada_b):
    nb, d = c.shape
    depth, _, n = ada_w.shape
    tn = 512
    return pl.pallas_call(
        _ada_kernel,
        grid=(depth, n // tn),
        in_specs=[
            pl.BlockSpec((nb, d), lambda l, j: (0, 0)),
            pl.BlockSpec((1, d, tn), lambda l, j: (l, 0, j)),
            pl.BlockSpec((1, 1, tn), lambda l, j: (l, 0, j)),
        ],
        out_specs=pl.BlockSpec((1, nb, tn), lambda l, j: (l, 0, j)),
        out_shape=jax.ShapeDtypeStruct((depth, nb, n), F32),
        compiler_params=_params("parallel", "parallel"),
        name="ada_mod",
    )(c, ada_w, ada_b.reshape(depth, 1, n))


def _modnorm(x, g, sc, sh):
    ms = jnp.mean(x * x, axis=-1, keepdims=True)
    return x * lax.rsqrt(ms + NORM_EPS) * g * (1.0 + sc) + sh


def _store_with_rope(y, o_ref, j, tn, rope, tabs):
    if rope is None:
        o_ref[0] = y.astype(o_ref.dtype)
        return
    lo, hi, shift, stride, phase = rope
    c_ref, s1_ref, s2_ref = tabs
    cpt = tn // LANES
    for c in range(cpt):
        gc = j * cpt + c
        yc = y[:, c * LANES:(c + 1) * LANES]
        is_rope = jnp.logical_and(jnp.logical_and(gc >= lo, gc < hi), gc % stride == phase)

        @pl.when(is_rope)
        def _(yc=yc, c=c):
            o_ref[0, :, c * LANES:(c + 1) * LANES] = _rope128(
                yc, c_ref[...], s1_ref[...], s2_ref[...], shift).astype(o_ref.dtype)

        @pl.when(jnp.logical_not(is_rope))
        def _(yc=yc, c=c):
            o_ref[0, :, c * LANES:(c + 1) * LANES] = yc.astype(o_ref.dtype)


def _modmm_kernel(x_ref, g_ref, sc_ref, sh_ref, w_ref, *rest, tn, rope):
    tabs, (o_ref, h_ref) = rest[:-2], rest[-2:]
    j = pl.program_id(2)

    @pl.when(j == 0)
    def _():
        h_ref[...] = _modnorm(x_ref[0], g_ref[...], sc_ref[0], sh_ref[0]).astype(BF16)

    y = jnp.dot(h_ref[...], w_ref[...], preferred_element_type=F32)
    _store_with_rope(y, o_ref, j, tn, rope, tabs)


def modmm(x, g, scale, shift, w, *, ts, tn, out_dtype=BF16, rope=None, tabs=()):
    b, s, d = x.shape
    n = w.shape[1]
    w_spec = pl.BlockSpec((d, tn), lambda bi, i, j: (0, j))
    tab_specs = [pl.BlockSpec((ts, LANES), lambda bi, i, j: (i, 0)) for _ in tabs]
    return pl.pallas_call(
        functools.partial(_modmm_kernel, tn=tn, rope=rope),
        grid=(b, s // ts, n // tn),
        in_specs=[
            pl.BlockSpec((1, ts, d), lambda bi, i, j: (bi, i, 0)),
            pl.BlockSpec((1, d), lambda bi, i, j: (0, 0)),
            pl.BlockSpec((1, 1, d), lambda bi, i, j: (bi, 0, 0)),
            pl.BlockSpec((1, 1, d), lambda bi, i, j: (bi, 0, 0)),
            w_spec,
        ] + tab_specs,
        out_specs=pl.BlockSpec((1, ts, tn), lambda bi, i, j: (bi, i, j)),
        out_shape=jax.ShapeDtypeStruct((b, s, n), out_dtype),
        scratch_shapes=[pltpu.VMEM((ts, d), BF16)],
        compiler_params=_params("parallel", "parallel", "arbitrary"),
        name="modmm",
    )(x, g, scale, shift, w, *tabs)


def _normmm_kernel(x_ref, g_ref, w_ref, *rest, tn, rope):
    tabs, (o_ref, h_ref) = rest[:-2], rest[-2:]
    j = pl.program_id(2)

    @pl.when(j == 0)
    def _():
        x = x_ref[0].astype(F32)
        ms = jnp.mean(x * x, axis=-1, keepdims=True)
        h_ref[...] = (x * lax.rsqrt(ms + NORM_EPS) * g_ref[...]).astype(BF16)

    y = jnp.dot(h_ref[...], w_ref[...], preferred_element_type=F32)
    _store_with_rope(y, o_ref, j, tn, rope, tabs)


def normmm(z, xcol, g, w, *, ts, tn, rope=None, tabs=()):
    b, s, _ = z.shape
    k, n = w.shape
    tab_specs = [pl.BlockSpec((ts, LANES), lambda bi, i, j: (i, 0)) for _ in tabs]
    return pl.pallas_call(
        functools.partial(_normmm_kernel, tn=tn, rope=rope),
        grid=(b, s // ts, n // tn),
        in_specs=[
            pl.BlockSpec((1, ts, k), lambda bi, i, j: (bi, i, xcol)),
            pl.BlockSpec((1, k), lambda bi, i, j: (0, 0)),
            pl.BlockSpec((k, tn), lambda bi, i, j: (0, j)),
        ] + tab_specs,
        out_specs=pl.BlockSpec((1, ts, tn), lambda bi, i, j: (bi, i, j)),
        out_shape=jax.ShapeDtypeStruct((b, s, n), BF16),
        scratch_shapes=[pltpu.VMEM((ts, k), BF16)],
        compiler_params=_params("parallel", "parallel", "arbitrary"),
        name="normmm",
    )(z, g, w, *tabs)


LOG2E = math.log2(math.e)


def _flash(q, k_ref, v_ref, m_sc, l_sc, acc_sc, tk):
    seq = k_ref.shape[0]
    m_sc[...] = jnp.full(m_sc.shape, -jnp.inf, F32)
    l_sc[...] = jnp.zeros(l_sc.shape, F32)
    acc_sc[...] = jnp.zeros(acc_sc.shape, F32)
    nck = tk // LANES
    nchunk = seq // tk

    def scores(c):
        return lax.dot_general(q, k_ref[c * tk:(c + 1) * tk, :], (((1,), (1,)), ((), ())),
                               preferred_element_type=F32)

    s_next = scores(0)
    for c in range(nchunk):
        s = s_next
        if c + 1 < nchunk:
            s_next = scores(c + 1)
        sc = [s[:, i * LANES:(i + 1) * LANES] for i in range(nck)]
        m_part = functools.reduce(jnp.maximum, sc)
        m_prev = m_sc[...]
        m_new = jnp.maximum(m_prev, jnp.max(m_part, axis=-1, keepdims=True))
        alpha = jnp.exp2(m_prev - m_new)
        pc = [jnp.exp2(x - m_new) for x in sc]
        l_sc[...] = alpha * l_sc[...] + functools.reduce(jnp.add, pc)
        p = jnp.concatenate([x.astype(BF16) for x in pc], axis=1)
        acc_sc[...] = alpha * acc_sc[...] + jnp.dot(
            p, v_ref[c * tk:(c + 1) * tk, :], preferred_element_type=F32)
        m_sc[...] = m_new


def _row_total(l):
    return jnp.sum(l, axis=-1, keepdims=True)


def _mla_attn_kernel(q_ref, kn_ref, v_ref, kpe_ref, o_ref, k_sc, m_sc, l_sc, acc_sc, *, tk, scale):
    @pl.when(pl.program_id(2) == 0)
    def _():
        k_sc[:, :LANES] = kn_ref[0]
        k_sc[:, LANES:] = kpe_ref[0]

    q = (q_ref[0].astype(F32) * (scale * LOG2E)).astype(BF16)
    _flash(q, k_sc, v_ref.at[0], m_sc, l_sc, acc_sc, tk)
    o_ref[0] = (acc_sc[...] / _row_total(l_sc[...])).astype(o_ref.dtype)


def mla_attention(q, kv, z, kpe_col, *, tq, tk):
    b, s, _ = q.shape
    scale = (MLA_NOPE + MLA_ROPE) ** -0.5
    return pl.pallas_call(
        functools.partial(_mla_attn_kernel, tk=tk, scale=scale),
        grid=(b, MLA_HEADS, s // tq),
        in_specs=[
            pl.BlockSpec((1, tq, MLA_QK_PAD), lambda bi, h, i: (bi, i, h)),
            pl.BlockSpec((1, s, LANES), lambda bi, h, i: (bi, 0, 2 * h)),
            pl.BlockSpec((1, s, LANES), lambda bi, h, i: (bi, 0, 2 * h + 1)),
            pl.BlockSpec((1, s, LANES), lambda bi, h, i: (bi, 0, kpe_col)),
        ],
        out_specs=pl.BlockSpec((1, tq, MLA_V), lambda bi, h, i: (bi, i, h)),
        out_shape=jax.ShapeDtypeStruct((b, s, MLA_HEADS * MLA_V), BF16),
        scratch_shapes=[
            pltpu.VMEM((s, MLA_QK_PAD), BF16),
            pltpu.VMEM((tq, LANES), F32),
            pltpu.VMEM((tq, LANES), F32),
            pltpu.VMEM((tq, MLA_V), F32),
        ],
        compiler_params=_params("parallel", "parallel", "arbitrary"),
        name="mla_attention",
    )(q, kv, kv, z)


def _diff_attn_kernel(q_ref, k_ref, v_ref, lam_ref, g_ref, o_ref, m_sc, l_sc, acc_sc,
                      *, tq, tk, scale, lam_init):
    q = q_ref[0].astype(F32) * (scale * LOG2E)
    lane = lax.broadcasted_iota(jnp.int32, q.shape, 1)
    q1 = jnp.where(lane < DIFF_HD, q, 0.0).astype(BF16)
    q2 = jnp.where(lane >= DIFF_HD, q, 0.0).astype(BF16)
    _flash(jnp.concatenate([q1, q2], axis=0), k_ref.at[0], v_ref.at[0], m_sc, l_sc, acc_sc, tk)
    lam = lam_ref[...]
    lam_full = (jnp.exp(jnp.sum(lam[0:1] * lam[1:2], axis=-1, keepdims=True))
                - jnp.exp(jnp.sum(lam[2:3] * lam[3:4], axis=-1, keepdims=True)) + lam_init)
    o1 = acc_sc[:tq, :] / _row_total(l_sc[:tq, :])
    o2 = acc_sc[tq:, :] / _row_total(l_sc[tq:, :])
    o = o1 - lam_full * o2
    ms = jnp.mean(o * o, axis=-1, keepdims=True)
    o = o * lax.rsqrt(ms + DIFF_SUBLN_EPS) * g_ref[...] * (1.0 - lam_init)
    o_ref[0] = o.astype(o_ref.dtype)


def diff_attention(z, lam, subln_g, lam_init, *, tq, tk):
    b, s, _ = z.shape
    scale = DIFF_HD ** -0.5
    nh = DIFF_HEADS
    return pl.pallas_call(
        functools.partial(_diff_attn_kernel, tq=tq, tk=tk, scale=scale, lam_init=lam_init),
        grid=(b, nh, s // tq),
        in_specs=[
            pl.BlockSpec((1, tq, LANES), lambda bi, h, i: (bi, i, h)),
            pl.BlockSpec((1, s, LANES), lambda bi, h, i: (bi, 0, nh + h)),
            pl.BlockSpec((1, s, LANES), lambda bi, h, i: (bi, 0, 2 * nh + h)),
            pl.BlockSpec((4, DIFF_HD), lambda bi, h, i: (0, 0)),
            pl.BlockSpec((1, LANES), lambda bi, h, i: (0, 0)),
        ],
        out_specs=pl.BlockSpec((1, tq, LANES), lambda bi, h, i: (bi, i, h)),
        out_shape=jax.ShapeDtypeStruct((b, s, nh * LANES), BF16),
        scratch_shapes=[
            pltpu.VMEM((2 * tq, LANES), F32),
            pltpu.VMEM((2 * tq, LANES), F32),
            pltpu.VMEM((2 * tq, LANES), F32),
        ],
        compiler_params=_params("parallel", "parallel", "arbitrary"),
        name="diff_attention",
    )(z, z, z, lam, subln_g.reshape(1, LANES))


def _finish(a, w_ref, x_ref, gm_ref, gp_ref, o_ref):
    y = jnp.dot(a.astype(BF16), w_ref[...], preferred_element_type=F32)
    ms = jnp.mean(y * y, axis=-1, keepdims=True)
    o_ref[0] = x_ref[0] + gm_ref[0] * (y * lax.rsqrt(ms + NORM_EPS) * gp_ref[...])


def _outmm_kernel(a_ref, g_ref, w_ref, x_ref, gm_ref, gp_ref, o_ref):
    a = a_ref[0].astype(F32) * _silu(g_ref[0].astype(F32))
    _finish(a, w_ref, x_ref, gm_ref, gp_ref, o_ref)


def _outmm2_kernel(a1_ref, a2_ref, g_ref, w_ref, x_ref, gm_ref, gp_ref, o_ref):
    a = (a1_ref[0] + a2_ref[0]) * _silu(g_ref[0].astype(F32))
    _finish(a, w_ref, x_ref, gm_ref, gp_ref, o_ref)


def outmm(acts, zgate, gcol, w, x, gate_mod, g_post, *, ts):
    b, s, d = x.shape
    row = lambda bi, i: (bi, i, 0)
    kern = _outmm_kernel if len(acts) == 1 else _outmm2_kernel
    return pl.pallas_call(
        kern,
        grid=(b, s // ts),
        in_specs=[pl.BlockSpec((1, ts, d), row) for _ in acts] + [
            pl.BlockSpec((1, ts, d), lambda bi, i: (bi, i, gcol)),
            pl.BlockSpec((d, d), lambda bi, i: (0, 0), pipeline_mode=pl.Buffered(1)),
            pl.BlockSpec((1, ts, d), row),
            pl.BlockSpec((1, 1, d), lambda bi, i: (bi, 0, 0)),
            pl.BlockSpec((1, d), lambda bi, i: (0, 0)),
        ],
        out_specs=pl.BlockSpec((1, ts, d), row),
        out_shape=jax.ShapeDtypeStruct((b, s, d), F32),
        compiler_params=_params("parallel", "parallel"),
        name="outmm",
    )(*acts, zgate, w, x, gate_mod, g_post)


def _lru_prepare(xe_sc, x_ref, prev_ref, next_ref, is_first, is_last, cw_ref, cb_ref, gw_ref,
                 gb_ref, sp_ref, a_sc, u_sc, d, ts, edge_row):
    w = LRU_WIDTH
    h = LRU_HALO
    xe_sc[pl.ds(h, ts), :] = x_ref[0].astype(F32)

    @pl.when(is_first)
    def _():
        xe_sc[pl.ds(0, h), :] = jnp.zeros((h, w), F32)

    @pl.when(jnp.logical_not(is_first))
    def _():
        xe_sc[pl.ds(0, h), :] = prev_ref[0].astype(F32)

    @pl.when(is_last)
    def _():
        xe_sc[pl.ds(h + ts, h), :] = jnp.zeros((h, w), F32)

    @pl.when(jnp.logical_not(is_last))
    def _():
        xe_sc[pl.ds(h + ts, h), :] = next_ref[0].astype(F32)

    xc = cb_ref[...] + sum(cw_ref[i:i + 1, :] * xe_sc[pl.ds(h - 1 + i, ts), :] for i in range(4))
    row = lax.broadcasted_iota(jnp.int32, (ts, LRU_BW), 0)
    for n in range(LRU_BLOCKS):
        cols = slice(n * LRU_BW, (n + 1) * LRU_BW)
        xn = xc[:, cols]
        g = jnp.dot(xn.astype(BF16), gw_ref[d, n], preferred_element_type=F32)
        r_t = jax.nn.sigmoid(g[:, :LRU_BW] + gb_ref[2 * d:2 * d + 1, cols])
        i_t = jax.nn.sigmoid(g[:, LRU_BW:] + gb_ref[2 * d + 1:2 * d + 2, cols])
        log_a = -LRU_C * r_t * sp_ref[d:d + 1, cols]
        a_t = jnp.exp(log_a)
        mult = jnp.sqrt(-jnp.tanh(log_a) * (1.0 + a_t * a_t))
        mult = jnp.where(row == edge_row, 1.0, mult)
        a_sc[:, cols] = a_t
        u_sc[:, cols] = mult * i_t * xn


def _lru_kernel(xf_ref, pf_ref, nf_ref, xb_ref, pb_ref, nb_ref, cw_ref, cb_ref, gw_ref, gb_ref,
                lam_ref, yf_ref, yb_ref, xe_sc, af_sc, uf_sc, ab_sc, ub_sc, hf_sc, hb_sc, sp_sc,
                *, ts):
    i = pl.program_id(1)
    nt = pl.num_programs(1)

    @pl.when(i == 0)
    def _():
        hf_sc[...] = jnp.zeros(hf_sc.shape, F32)
        hb_sc[...] = jnp.zeros(hb_sc.shape, F32)
        nl = -lam_ref[...]
        sp_sc[...] = jnp.maximum(nl, 0.0) + jnp.log(1.0 + jnp.exp(-jnp.abs(nl)))

    first, last = i == 0, i == nt - 1
    _lru_prepare(xe_sc, xf_ref, pf_ref, nf_ref, first, last, cw_ref, cb_ref, gw_ref, gb_ref,
                 sp_sc, af_sc, uf_sc, 0, ts, jnp.where(first, 0, -1))
    _lru_prepare(xe_sc, xb_ref, pb_ref, nb_ref, last, first, cw_ref, cb_ref, gw_ref, gb_ref,
                 sp_sc, ab_sc, ub_sc, 1, ts, jnp.where(first, ts - 1, -1))

    def step(k, carry):
        hf, hb = carry
        kb = ts - 1 - k
        hf = af_sc[pl.ds(k, 1), :] * hf + uf_sc[pl.ds(k, 1), :]
        hb = ab_sc[pl.ds(kb, 1), :] * hb + ub_sc[pl.ds(kb, 1), :]
        yf_ref[0, pl.ds(k, 1), :] = hf
        yb_ref[0, pl.ds(kb, 1), :] = hb
        return hf, hb

    hf, hb = lax.fori_loop(0, ts, step, (hf_sc[...], hb_sc[...]), unroll=8)
    hf_sc[...] = hf
    hb_sc[...] = hb


def lru_scan(z, conv_w, conv_b, gate_w, gate_b, lam, *, ts):
    b, s, _ = z.shape
    w = LRU_WIDTH
    h = LRU_HALO
    nt = s // ts
    rpb = ts // h
    nhb = s // h
    fwd = lambda bi, i: (bi, i, 0)
    bwd = lambda bi, i: (bi, nt - 1 - i, 0)
    fwd_prev = lambda bi, i: (bi, jnp.maximum(i * rpb - 1, 0), 0)
    fwd_next = lambda bi, i: (bi, jnp.minimum((i + 1) * rpb, nhb - 1), 0)
    bwd_prev = lambda bi, i: (bi, jnp.maximum((nt - 1 - i) * rpb - 1, 0), 0)
    bwd_next = lambda bi, i: (bi, jnp.minimum((nt - i) * rpb, nhb - 1), 0)
    full = lambda *shape: pl.BlockSpec(shape, lambda bi, i: (0,) * len(shape))
    return pl.pallas_call(
        functools.partial(_lru_kernel, ts=ts),
        grid=(b, nt),
        in_specs=[
            pl.BlockSpec((1, ts, w), fwd), pl.BlockSpec((1, h, w), fwd_prev),
            pl.BlockSpec((1, h, w), fwd_next),
            pl.BlockSpec((1, ts, w), bwd), pl.BlockSpec((1, h, w), bwd_prev),
            pl.BlockSpec((1, h, w), bwd_next),
            full(4, w), full(1, w), full(2, LRU_BLOCKS, LRU_BW, 2 * LRU_BW), full(4, w), full(2, w),
        ],
        out_specs=[pl.BlockSpec((1, ts, w), fwd), pl.BlockSpec((1, ts, w), bwd)],
        out_shape=[jax.ShapeDtypeStruct((b, s, w), F32)] * 2,
        scratch_shapes=[
            pltpu.VMEM((ts + 2 * h, w), F32),
            pltpu.VMEM((ts, w), F32), pltpu.VMEM((ts, w), F32),
            pltpu.VMEM((ts, w), F32), pltpu.VMEM((ts, w), F32),
            pltpu.VMEM((1, w), F32), pltpu.VMEM((1, w), F32), pltpu.VMEM((2, w), F32),
        ],
        compiler_params=_params("parallel", "arbitrary"),
        name="lru_scan",
    )(z, z, z, z, z, z, conv_w, conv_b, gate_w, gate_b, lam)


RWKV_HALO = 8


def _rwkv_shifted(x_ref, p_ref, n_ref, g_ref, sc_ref, sh_ref, he_sc, is_first, is_last, ts):
    hl = RWKV_HALO
    d = D_MODEL
    g, sc, sh = g_ref[...], sc_ref[0], sh_ref[0]
    he_sc[pl.ds(hl, ts), :] = _modnorm(x_ref[0], g, sc, sh)

    @pl.when(is_first)
    def _():
        he_sc[pl.ds(0, hl), :] = jnp.zeros((hl, d), F32)

    @pl.when(jnp.logical_not(is_first))
    def _():
        he_sc[pl.ds(0, hl), :] = _modnorm(p_ref[0], g, sc, sh)

    @pl.when(is_last)
    def _():
        he_sc[pl.ds(hl + ts, hl), :] = jnp.zeros((hl, d), F32)

    @pl.when(jnp.logical_not(is_last))
    def _():
        he_sc[pl.ds(hl + ts, hl), :] = _modnorm(n_ref[0], g, sc, sh)

    h = he_sc[pl.ds(hl, ts), :]
    xx = 0.5 * (he_sc[pl.ds(hl - 1, ts), :] + he_sc[pl.ds(hl + 1, ts), :]) - h
    return h, xx


def _rwkv_proj_kernel(x_ref, p_ref, n_ref, g_ref, sc_ref, sh_ref, mu_ref, w_ref, o_ref,
                      he_sc, xx_sc, *, ts):
    i, m = pl.program_id(1), pl.program_id(2)

    @pl.when(m == 0)
    def _():
        _, xx = _rwkv_shifted(x_ref, p_ref, n_ref, g_ref, sc_ref, sh_ref, he_sc,
                              i == 0, i == pl.num_programs(1) - 1, ts)
        xx_sc[...] = xx

    xs = he_sc[pl.ds(RWKV_HALO, ts), :] + xx_sc[...] * mu_ref[0]
    o_ref[0] = jnp.dot(xs.astype(BF16), w_ref[0], preferred_element_type=F32).astype(o_ref.dtype)


def _halo_specs(ts, s, halo, d):
    rpb, nhb = ts // halo, s // halo
    prev = lambda bi, i, *_: (bi, jnp.maximum(i * rpb - 1, 0), 0)
    nxt = lambda bi, i, *_: (bi, jnp.minimum((i + 1) * rpb, nhb - 1), 0)
    return pl.BlockSpec((1, halo, d), prev), pl.BlockSpec((1, halo, d), nxt)


def rwkv_proj(x, g, scale, shift, mu, w_in, *, ts):
    b, s, d = x.shape
    prev, nxt = _halo_specs(ts, s, RWKV_HALO, d)
    return pl.pallas_call(
        functools.partial(_rwkv_proj_kernel, ts=ts),
        grid=(b, s // ts, 4),
        in_specs=[
            pl.BlockSpec((1, ts, d), lambda bi, i, m: (bi, i, 0)), prev, nxt,
            pl.BlockSpec((1, d), lambda bi, i, m: (0, 0)),
            pl.BlockSpec((1, 1, d), lambda bi, i, m: (bi, 0, 0)),
            pl.BlockSpec((1, 1, d), lambda bi, i, m: (bi, 0, 0)),
            pl.BlockSpec((1, 1, d), lambda bi, i, m: (m, 0, 0)),
            pl.BlockSpec((1, d, d), lambda bi, i, m: (m, 0, 0)),
        ],
        out_specs=pl.BlockSpec((1, ts, d), lambda bi, i, m: (bi, i, m)),
        out_shape=jax.ShapeDtypeStruct((b, s, 4 * d), BF16),
        scratch_shapes=[pltpu.VMEM((ts + 2 * RWKV_HALO, d), F32), pltpu.VMEM((ts, d), F32)],
        compiler_params=_params("parallel", "parallel", "arbitrary"),
        name="rwkv_proj",
    )(x, x, x, g, scale, shift, mu.reshape(6, 1, d), w_in)


RWKV_DECAY_SCALE = math.exp(-0.5)


def _rwkv_lora_kernel(x_ref, p_ref, n_ref, g_ref, sc_ref, sh_ref, mu_ref, w0_ref, w1_ref, w2_ref,
                      a0_ref, a1_ref, a2_ref, lw_ref, a_ref, he_sc, *, ts):
    i = pl.program_id(1)
    h, xx = _rwkv_shifted(x_ref, p_ref, n_ref, g_ref, sc_ref, sh_ref, he_sc,
                          i == 0, i == pl.num_programs(1) - 1, ts)
    xw = (h + xx * mu_ref[4]).astype(BF16)
    xa = (h + xx * mu_ref[5]).astype(BF16)
    for dr in range(2):
        t1 = jnp.tanh(jnp.dot(xw, w1_ref[dr], preferred_element_type=F32))
        wl = w0_ref[dr] + jnp.dot(t1.astype(BF16), w2_ref[dr], preferred_element_type=F32)
        lw_ref[dr, 0] = -RWKV_DECAY_SCALE * jax.nn.sigmoid(wl)
        t2 = jnp.dot(xa, a1_ref[dr], preferred_element_type=F32)
        al = a0_ref[dr] + jnp.dot(t2.astype(BF16), a2_ref[dr], preferred_element_type=F32)
        a_ref[dr, 0] = jax.nn.sigmoid(al).astype(a_ref.dtype)


def rwkv_lora(x, g, scale, shift, mu, w0, w1, w2, a0, a1, a2, *, ts):
    b, s, d = x.shape
    lp = RWKV_LORA_PAD
    prev, nxt = _halo_specs(ts, s, RWKV_HALO, d)
    full = lambda *shape: pl.BlockSpec(shape, lambda bi, i: (0,) * len(shape))
    out = pl.BlockSpec((2, 1, ts, d), lambda bi, i: (0, bi, i, 0))
    return pl.pallas_call(
        functools.partial(_rwkv_lora_kernel, ts=ts),
        grid=(b, s // ts),
        in_specs=[
            pl.BlockSpec((1, ts, d), lambda bi, i: (bi, i, 0)), prev, nxt,
            full(1, d),
            pl.BlockSpec((1, 1, d), lambda bi, i: (bi, 0, 0)),
            pl.BlockSpec((1, 1, d), lambda bi, i: (bi, 0, 0)),
            full(6, 1, d), full(2, 1, d), full(2, d, lp), full(2, lp, d),
            full(2, 1, d), full(2, d, lp), full(2, lp, d),
        ],
        out_specs=[out, out],
        out_shape=[jax.ShapeDtypeStruct((2, b, s, d), F32), jax.ShapeDtypeStruct((2, b, s, d), BF16)],
        scratch_shapes=[pltpu.VMEM((ts + 2 * RWKV_HALO, d), F32)],
        compiler_params=_params("parallel", "parallel"),
        name="rwkv_lora",
    )(x, x, x, g, scale, shift, mu.reshape(6, 1, d), w0.reshape(2, 1, d), w1, w2,
      a0.reshape(2, 1, d), a1, a2)


def _hilo(x):
    hi = x.astype(BF16)
    return hi, (x - hi.astype(F32)).astype(BF16)


def _dot2(x, w):
    hi, lo = _hilo(x)
    return (jnp.dot(hi, w, preferred_element_type=F32) + jnp.dot(lo, w, preferred_element_type=F32))


def _nt(x, y):
    return lax.dot_general(x, y, (((1,), (1,)), ((), ())), preferred_element_type=F32)


def _bdot(x, y):
    return jnp.dot(x.astype(BF16), y.astype(BF16), preferred_element_type=F32)


def _rwkv_masks(reverse):
    t = RWKV_CHUNK
    n = RWKV_N
    lane_t = lax.broadcasted_iota(jnp.int32, (t, LANES), 1)
    row_t = lax.broadcasted_iota(jnp.int32, (t, LANES), 0)
    col_t = lane_t % n
    r128 = lax.broadcasted_iota(jnp.int32, (LANES, LANES), 0)
    c128 = lax.broadcasted_iota(jnp.int32, (LANES, LANES), 1)
    rt = lax.broadcasted_iota(jnp.int32, (t, t), 0)
    ct = lax.broadcasted_iota(jnp.int32, (t, t), 1)
    if reverse:
        strict2, incl2, tri = col_t > row_t, col_t >= row_t, ct >= rt
    else:
        strict2, incl2, tri = col_t < row_t, col_t <= row_t, ct <= rt
    same_head = (r128 // n) == (c128 // n)
    levels = []
    sz = 1
    while sz < t:
        hi_r, lo_c = (r128 % (2 * sz)) >= sz, (c128 % (2 * sz)) < sz
        if reverse:
            off = jnp.logical_and(jnp.logical_not(hi_r), jnp.logical_not(lo_c))
        else:
            off = jnp.logical_and(hi_r, lo_c)
        levels.append(jnp.logical_and(off, (r128 // (2 * sz)) == (c128 // (2 * sz))))
        sz *= 2
    return dict(m1=lane_t < n, strict2=strict2, incl2=incl2,
                tri=jnp.where(tri, 1.0, 0.0).astype(BF16), same_head=same_head,
                ones_bd=jnp.where(same_head, 1.0, 0.0).astype(BF16),
                eye=jnp.where(r128 == c128, 1.0, 0.0), levels=levels)


def _each(f, *lists):
    return [f(*xs) for xs in zip(*lists)]


def _rwkv_chunk_terms(lw, r, k, v, a, kk_scale, ka_scale, rk_scale, mk, reverse):
    t = RWKV_CHUNK
    m1, strict2, incl2, tri, same_head = mk['m1'], mk['strict2'], mk['incl2'], mk['tri'], mk['same_head']

    def sel(top, bot):
        return jnp.where(m1, top, bot)

    def stack(x):
        return jnp.concatenate([jnp.where(m1, x, 0.0), jnp.where(m1, 0.0, x)], axis=0)

    kk = _each(jnp.multiply, k, kk_scale)
    kd = _each(lambda k_, a_, s_: k_ * (1.0 + (a_ - 1.0) * s_), k, a, ka_scale)
    seg = _each(lambda kk_, r_, kd_, s_: _bdot(
        jnp.concatenate([kk_ * kk_, r_ * kd_ * s_], axis=0), mk['ones_bd']), kk, r, kd, rk_scale)
    kk = _each(lambda kk_, sg: kk_ / jnp.maximum(jnp.sqrt(sg[:t]), 1e-12), kk, seg)
    bonus = _each(lambda sg, v_: sg[t:] * v_, seg, v)

    c = _each(lambda lw_: _dot2_left(tri, lw_), lw)
    e_pos = _each(jnp.exp, c)
    e_neg = _each(lambda c_: jnp.exp(-c_), c)
    at = _each(lambda kk_, c_, lw_: -kk_ * jnp.exp(c_ - lw_), kk, c, lw)
    rt_ = _each(jnp.multiply, r, e_pos)
    bt = _each(lambda kk_, a_, e_: kk_ * a_ * e_, kk, a, e_neg)
    kt = _each(jnp.multiply, kd, e_neg)
    big = _each(lambda at_, rt__, bt_, kt_: _nt(
        jnp.concatenate([at_, rt__], axis=0).astype(BF16),
        jnp.concatenate([stack(bt_), stack(kt_)], axis=0).astype(BF16)), at, rt_, bt, kt)
    aak = _each(lambda b_: jnp.where(strict2, b_[:t, LANES:], 0.0), big)
    rb = _each(lambda b_: jnp.where(incl2, b_[t:, :LANES], 0.0), big)
    rk = _each(lambda b_: jnp.where(incl2, b_[t:, LANES:], 0.0), big)

    abd = _each(lambda b_: stack(jnp.where(strict2, b_[:t, :LANES], 0.0)), big)
    dinv = _each(lambda ab: mk['eye'] + jnp.where(mk['levels'][0], ab, 0.0), abd)
    for off in mk['levels'][1:]:
        dinv_b = _each(lambda d_: d_.astype(BF16), dinv)
        half = _each(lambda db, ab: jnp.dot(db, jnp.where(off, ab, 0.0).astype(BF16),
                                            preferred_element_type=F32), dinv_b, abd)
        dinv = _each(lambda d_, h_, db: d_ + jnp.dot(h_.astype(BF16), db, preferred_element_type=F32),
                     dinv, half, dinv_b)

    vst = _each(stack, v)
    aakv = _each(_bdot, aak, vst)
    x = _each(lambda d_, at_, av: _bdot(d_, jnp.concatenate(
        [jnp.concatenate([at_, av], axis=1)] * 2, axis=0)), dinv, at, aakv)
    atp = _each(lambda x_: sel(x_[:t, :LANES], x_[t:, :LANES]), x)
    u0 = _each(lambda x_: sel(x_[:t, LANES:], x_[t:, LANES:]), x)
    y = _each(lambda rb_, p_, u_: _bdot(rb_, jnp.concatenate([stack(p_), stack(u_)], axis=1)),
              rb, atp, u0)
    rkv = _each(_bdot, rk, vst)
    rp = _each(lambda rt__, y_: rt__ + y_[:, :LANES], rt_, y)
    o0 = _each(lambda y_, q_: y_[:, LANES:] + q_, y, rkv)
    gmt = _each(lambda p_, bt_: jnp.where(same_head, _bdot(p_.T, bt_), 0.0), atp, bt)
    zty = _each(lambda u_, v_, bt_, kt_: jnp.where(same_head, _bdot(
        jnp.concatenate([u_, v_], axis=0).T, jnp.concatenate([bt_, kt_], axis=0)), 0.0),
        u0, v, bt, kt)
    decay_row = _each(lambda e_: e_[0:1, :] if reverse else e_[t - 1:t, :], e_pos)
    return list(zip(rp, o0, gmt, zty, decay_row, bonus))


def _dot2_left(w, x):
    hi, lo = _hilo(x)
    return (jnp.dot(w, hi, preferred_element_type=F32) + jnp.dot(w, lo, preferred_element_type=F32))


def _rwkv_scan_kernel(lw_ref, a_ref, r_ref, k_ref, v_ref, kk_ref, ka_ref, rk_ref, o_ref, bo_ref,
                      s_sc, *, ts, npairs, reverse):
    @pl.when(pl.program_id(2) == 0)
    def _():
        s_sc[...] = jnp.zeros(s_sc.shape, F32)

    t = RWKV_CHUNK
    nch = ts // t
    order = range(nch - 1, -1, -1) if reverse else range(nch)
    mk = _rwkv_masks(reverse)
    chains = [(ci, hp) for ci in order for hp in range(npairs)]
    rows = lambda ci: slice(ci * t, (ci + 1) * t)
    cols = lambda hp: slice(hp * LANES, (hp + 1) * LANES)
    blk = lambda ref: [ref[0, rows(ci), cols(hp)].astype(F32) for ci, hp in chains]
    blk4 = lambda ref: [ref[0, 0, rows(ci), cols(hp)].astype(F32) for ci, hp in chains]
    par = lambda ref: [ref[:, cols(hp)] for _, hp in chains]
    terms = dict(zip(chains, _rwkv_chunk_terms(
        blk4(lw_ref), blk(r_ref), blk(k_ref), blk(v_ref), blk4(a_ref),
        par(kk_ref), par(ka_ref), par(rk_ref), mk, reverse)))
    s = [s_sc[hp] for hp in range(npairs)]
    for ci in order:
        for hp in range(npairs):
            rp, o0, gmt, zty, decay_row, bonus = terms[ci, hp]
            s_b = s[hp].astype(BF16)
            o_ref[0, rows(ci), cols(hp)] = o0 + _nt(rp.astype(BF16), s_b)
            bo_ref[0, rows(ci), cols(hp)] = bonus.astype(bo_ref.dtype)
            s[hp] = (s[hp] + jnp.dot(s_b, gmt.astype(BF16), preferred_element_type=F32)
                     + zty) * decay_row
    for hp in range(npairs):
        s_sc[hp] = s[hp]


def rwkv_scan(lw, a, rkvg, k_k, k_a, r_k, dr, *, ts, npairs):
    _, b, s, d = lw.shape
    nt = s // ts
    wd = npairs * LANES
    ng = d // wd
    reverse = dr == 1
    tmap = (lambda i: nt - 1 - i) if reverse else (lambda i: i)
    dspec = pl.BlockSpec((1, 1, ts, wd), lambda bi, g, i: (dr, bi, tmap(i), g))
    col = lambda m: pl.BlockSpec((1, ts, wd), lambda bi, g, i: (bi, tmap(i), m * ng + g))
    par = pl.BlockSpec((1, wd), lambda bi, g, i: (0, g))
    out = pl.BlockSpec((1, ts, wd), lambda bi, g, i: (bi, tmap(i), g))
    return pl.pallas_call(
        functools.partial(_rwkv_scan_kernel, ts=ts, npairs=npairs, reverse=reverse),
        grid=(b, ng, nt),
        in_specs=[dspec, dspec, col(0), col(1), col(2), par, par, par],
        out_specs=[out, out],
        out_shape=[jax.ShapeDtypeStruct((b, s, d), F32), jax.ShapeDtypeStruct((b, s, d), BF16)],
        scratch_shapes=[pltpu.VMEM((npairs, LANES, LANES), F32)],
        compiler_params=_params("parallel", "parallel", "arbitrary"),
        name="rwkv_scan_bwd" if reverse else "rwkv_scan_fwd",
    )(lw, a, rkvg, rkvg, rkvg, k_k, k_a, r_k)


def _rwkv_out_kernel(of_ref, ob_ref, bf_ref, bb_ref, g_ref, lg_ref, lb_ref, w_ref, x_ref, gm_ref,
                     gp_ref, o_ref, a_sc):
    r128 = lax.broadcasted_iota(jnp.int32, (LANES, LANES), 0)
    c128 = lax.broadcasted_iota(jnp.int32, (LANES, LANES), 1)
    avg = jnp.where((r128 // RWKV_N) == (c128 // RWKV_N), 1.0 / RWKV_N, 0.0).astype(BF16)
    for c in range(D_MODEL // LANES):
        cols = slice(c * LANES, (c + 1) * LANES)
        o = of_ref[0, :, cols] + ob_ref[0, :, cols]
        cen = o - _dot2(o, avg)
        var = _dot2(cen * cen, avg)
        on = cen * lax.rsqrt(var + RWKV_GN_EPS)
        y = (on * lg_ref[:, cols] + lb_ref[:, cols]
             + bf_ref[0, :, cols].astype(F32) + bb_ref[0, :, cols].astype(F32))
        a_sc[:, cols] = (y * _silu(g_ref[0, :, cols].astype(F32))).astype(BF16)
    _finish(a_sc[...], w_ref, x_ref, gm_ref, gp_ref, o_ref)


def rwkv_out(o_f, o_b, bo_f, bo_b, rkvg, ln_g, ln_b, w, x, gate_mod, g_post, *, ts):
    b, s, d = x.shape
    row = lambda bi, i: (bi, i, 0)
    vec = pl.BlockSpec((1, d), lambda bi, i: (0, 0))
    return pl.pallas_call(
        _rwkv_out_kernel,
        grid=(b, s // ts),
        in_specs=[pl.BlockSpec((1, ts, d), row)] * 4 + [
            pl.BlockSpec((1, ts, d), lambda bi, i: (bi, i, 3)), vec, vec,
            pl.BlockSpec((d, d), lambda bi, i: (0, 0)),
            pl.BlockSpec((1, ts, d), row),
            pl.BlockSpec((1, 1, d), lambda bi, i: (bi, 0, 0)),
            vec,
        ],
        out_specs=pl.BlockSpec((1, ts, d), row),
        out_shape=jax.ShapeDtypeStruct((b, s, d), F32),
        scratch_shapes=[pltpu.VMEM((ts, d), BF16)],
        compiler_params=_params("parallel", "parallel"),
        name="rwkv_out",
    )(o_f, o_b, bo_f, bo_b, rkvg, ln_g, ln_b, w, x, gate_mod, g_post)


def _prep_weights(p):
    d = D_MODEL
    w = {}
    wi = p['mla_w_in'][0]
    q_lat, kv_lat, k_pe, gate = jnp.split(
        wi, [MLA_Q_RANK, MLA_Q_RANK + MLA_KV_RANK, MLA_Q_RANK + MLA_KV_RANK + MLA_ROPE], axis=1)
    w['mla_in'] = jnp.concatenate(
        [gate, q_lat, kv_lat, k_pe, jnp.zeros((d, LANES - MLA_ROPE), F32)], axis=1).astype(BF16)
    wq = p['mla_w_q_up'][0].reshape(MLA_Q_RANK, MLA_HEADS, MLA_NOPE + MLA_ROPE)
    wq = jnp.pad(wq, ((0, 0), (0, 0), (0, MLA_QK_PAD - MLA_NOPE - MLA_ROPE)))
    w['mla_q'] = wq.reshape(MLA_Q_RANK, MLA_HEADS * MLA_QK_PAD).astype(BF16)
    w['mla_kv'] = p['mla_w_kv_up'][0].astype(BF16)
    w['mla_out'] = p['mla_w_out'][0].astype(BF16)
    w['diff_in'] = p['diff_w_in'][0].astype(BF16)
    w['diff_out'] = p['diff_w_out'][0].astype(BF16)
    w['lru_in'] = p['lru_w_in'][0].astype(BF16)
    w['lru_out'] = p['lru_w_out'][0].astype(BF16)
    gw = p['lru_gate_w'][0]
    w['lru_gate'] = jnp.concatenate([gw[:, 0], gw[:, 1]], axis=-1).astype(BF16)
    w['rwkv_in'] = p['rwkv_w_in'][0].astype(BF16)
    pad = RWKV_LORA_PAD - RWKV_LORA
    for nm in ('w1', 'a1'):
        w['rwkv_' + nm] = jnp.pad(p['rwkv_' + nm][0], ((0, 0), (0, 0), (0, pad))).astype(BF16)
    for nm in ('w2', 'a2'):
        w['rwkv_' + nm] = jnp.pad(p['rwkv_' + nm][0], ((0, 0), (0, pad), (0, 0))).astype(BF16)
    w['rwkv_out'] = p['rwkv_w_out'][0].astype(BF16)
    return w


def _mla_layer(x, pre_g, scale, shift, gate_mod, post_g, p, w, tabs):
    mla_tabs = tabs['mla']
    z = modmm(x, pre_g, scale, shift, w['mla_in'], ts=1024, tn=640,
              rope=(24, 25, MLA_ROPE // 2, 1, 0), tabs=mla_tabs)
    q = normmm(z, 4, p['mla_q_norm_g'], w['mla_q'], ts=1024, tn=512,
               rope=(0, 2 * MLA_HEADS, MLA_ROPE // 2, 2, 1), tabs=mla_tabs)
    kv = normmm(z, 5, p['mla_kv_norm_g'], w['mla_kv'], ts=1024, tn=512)
    o = mla_attention(q, kv, z, 24, tq=1024, tk=1024)
    return outmm([o], z, 0, w['mla_out'], x, gate_mod, post_g, ts=512)


def _diff_layer(x, layer_idx, pre_g, scale, shift, gate_mod, post_g, p, w, tabs):
    z = modmm(x, pre_g, scale, shift, w['diff_in'], ts=1024, tn=1024,
              rope=(0, 2 * DIFF_HEADS, DIFF_ROT // 2, 1, 0), tabs=tabs['diff'])
    lam_init = 0.8 - 0.6 * math.exp(-0.3 * layer_idx)
    o = diff_attention(z, p['diff_lambda'][0], p['diff_subln_g'][0], lam_init, tq=512, tk=1024)
    return outmm([o], z, 3, w['diff_out'], x, gate_mod, post_g, ts=512)


def _lru_layer(x, pre_g, scale, shift, gate_mod, post_g, p, w):
    z = modmm(x, pre_g, scale, shift, w['lru_in'], ts=1024, tn=1024)
    y_f, y_b = lru_scan(z, p['lru_conv_w'][0], p['lru_conv_b'], w['lru_gate'],
                        p['lru_gate_b'][0].reshape(4, LRU_WIDTH), p['lru_lambda'][0], ts=256)
    return outmm([y_f, y_b], z, 1, w['lru_out'], x, gate_mod, post_g, ts=256)


def _rwkv_layer(x, pre_g, scale, shift, gate_mod, post_g, p, w):
    mu = p['rwkv_mu'][0]
    rkvg = rwkv_proj(x, pre_g, scale, shift, mu, w['rwkv_in'], ts=512)
    lw, a = rwkv_lora(x, pre_g, scale, shift, mu, p['rwkv_w0'][0], w['rwkv_w1'], w['rwkv_w2'],
                      p['rwkv_a0'][0], w['rwkv_a1'], w['rwkv_a2'], ts=256)
    k_k = p['rwkv_k_k']
    k_a = p['rwkv_k_a']
    r_k = p['rwkv_r_k'].reshape(1, D_MODEL)
    o_f, bo_f = rwkv_scan(lw, a, rkvg, k_k, k_a, r_k, 0, ts=256, npairs=4)
    o_b, bo_b = rwkv_scan(lw, a, rkvg, k_k, k_a, r_k, 1, ts=256, npairs=4)
    return rwkv_out(o_f, o_b, bo_f, bo_b, rkvg, p['rwkv_ln_g'], p['rwkv_ln_b'], w['rwkv_out'],
                    x, gate_mod, post_g, ts=256)


def _trunk(x, mods, p, w):
    b, s, d = x.shape
    tabs = {
        'mla': _rope_tables(s, 1, LANES, MLA_ROPE, MLA_THETA, 0.0),
        'diff': _rope_tables(s, LANES // DIFF_HD, DIFF_HD, DIFF_ROT, ROPE_THETA, 1.0),
    }
    for i in range(DEPTH):
        shift, scale, gate_mod = (mods[i, :, k * d:(k + 1) * d].reshape(b, 1, d) for k in range(3))
        pre_g = p['norm_pre_g'][i:i + 1]
        post_g = p['norm_post_g'][i:i + 1]
        args = (pre_g, scale, shift, gate_mod, post_g, p, w)
        if i % 4 == 0:
            x = _mla_layer(x, *args, tabs)
        elif i % 4 == 1:
            x = _diff_layer(x, i, *args, tabs)
        elif i % 4 == 2:
            x = _lru_layer(x, *args)
        else:
            x = _rwkv_layer(x, *args)
    return x


def kernel(x_prompt, x_sample, c_prompt, c_sample, ada_w, ada_b, norm_pre_g, norm_post_g, mla_w_in, mla_q_norm_g, mla_kv_norm_g, mla_w_q_up, mla_w_kv_up, mla_w_out, diff_w_in, diff_lambda, diff_subln_g, diff_w_out, lru_w_in, lru_conv_w, lru_conv_b, lru_gate_w, lru_gate_b, lru_lambda, lru_w_out, rwkv_mu, rwkv_w_in, rwkv_w0, rwkv_w1, rwkv_w2, rwkv_a0, rwkv_a1, rwkv_a2, rwkv_k_k, rwkv_k_a, rwkv_r_k, rwkv_ln_g, rwkv_ln_b, rwkv_w_out):
    p = dict(
        norm_pre_g=norm_pre_g, norm_post_g=norm_post_g,
        mla_w_in=mla_w_in, mla_q_norm_g=mla_q_norm_g, mla_kv_norm_g=mla_kv_norm_g,
        mla_w_q_up=mla_w_q_up, mla_w_kv_up=mla_w_kv_up, mla_w_out=mla_w_out,
        diff_w_in=diff_w_in, diff_lambda=diff_lambda, diff_subln_g=diff_subln_g,
        diff_w_out=diff_w_out,
        lru_w_in=lru_w_in, lru_conv_w=lru_conv_w, lru_conv_b=lru_conv_b, lru_gate_w=lru_gate_w,
        lru_gate_b=lru_gate_b, lru_lambda=lru_lambda, lru_w_out=lru_w_out,
        rwkv_mu=rwkv_mu, rwkv_w_in=rwkv_w_in, rwkv_w0=rwkv_w0, rwkv_w1=rwkv_w1, rwkv_w2=rwkv_w2,
        rwkv_a0=rwkv_a0, rwkv_a1=rwkv_a1, rwkv_a2=rwkv_a2, rwkv_k_k=rwkv_k_k, rwkv_k_a=rwkv_k_a,
        rwkv_r_k=rwkv_r_k, rwkv_ln_g=rwkv_ln_g, rwkv_ln_b=rwkv_ln_b, rwkv_w_out=rwkv_w_out,
    )
    w = _prep_weights(p)
    nb = x_prompt.shape[0]
    mods = ada_mod(jnp.concatenate([c_prompt, c_sample], axis=0), ada_w, ada_b)
    y_prompt = _trunk(x_prompt, mods[:, :nb], p, w)
    y_sample = _trunk(x_sample, mods[:, nb:], p, w)
    return (y_prompt, y_sample)
```

```python
import functools
import math

import jax
import jax.numpy as jnp
from jax import lax
from jax.experimental import pallas as pl
from jax.experimental.pallas import tpu as pltpu

F32 = jnp.float32
BF16 = jnp.bfloat16

D_MODEL = 2048
DEPTH = 4
NORM_EPS = 1e-6
LANES = 128
VMEM_LIMIT = 56 * 1024 * 1024

MLA_HEADS = 16
MLA_Q_RANK = 512
MLA_KV_RANK = 512
MLA_NOPE = 128
MLA_ROPE = 64
MLA_V = 128
MLA_THETA = 10000.0
MLA_QK_PAD = 256

DIFF_HEADS = 16
DIFF_HD = 64
DIFF_ROT = 16
ROPE_THETA = 500000.0
DIFF_SUBLN_EPS = 1e-5

LRU_WIDTH = 2048
LRU_BLOCKS = 16
LRU_BW = 128
LRU_C = 8.0
LRU_HALO = 16

RWKV_N = 64
RWKV_HEADS = 32
RWKV_LORA = 96
RWKV_LORA_PAD = 128
RWKV_GN_EPS = 64e-5
RWKV_CHUNK = 64


def _params(*sem):
    return pltpu.CompilerParams(dimension_semantics=sem, vmem_limit_bytes=VMEM_LIMIT)


def _silu(x):
    return x * jax.nn.sigmoid(x)


def _rope128(y, c, s1, s2, shift):
    return y * c + pltpu.roll(y, LANES - shift, 1) * s1 + pltpu.roll(y, shift, 1) * s2


def _rope_tables(seq, heads_per_chunk, head_dim, rot_dim, theta, pad_cos):
    half = rot_dim // 2
    inv = 1.0 / (theta ** (jnp.arange(0, rot_dim, 2, dtype=F32) / rot_dim))
    ang = jnp.arange(seq, dtype=F32)[:, None] * inv[None, :]
    cos, sin = jnp.cos(ang), jnp.sin(ang)
    rest = head_dim - rot_dim
    c = jnp.concatenate([cos, cos, jnp.full((seq, rest), pad_cos, F32)], axis=-1)
    s1 = jnp.concatenate([-sin, jnp.zeros((seq, half + rest), F32)], axis=-1)
    s2 = jnp.concatenate([jnp.zeros((seq, half), F32), sin, jnp.zeros((seq, rest), F32)], axis=-1)
    tile = lambda t: jnp.tile(t, (1, heads_per_chunk))
    return tile(c), tile(s1), tile(s2)


def _ada_kernel(c_ref, w_ref, b_ref, o_ref):
    cs = _silu(c_ref[...]).astype(BF16)
    o_ref[0] = jnp.dot(cs, w_ref[0].astype(BF16), preferred_element_type=F32) + b_ref[0]


def ada_mod(c, ada_w, ada_b):
    nb, d = c.shape
    depth, _, n = ada_w.shape
    tn = 512
    return pl.pallas_call(
        _ada_kernel,
        grid=(depth, n // tn),
        in_specs=[
            pl.BlockSpec((nb, d), lambda l, j: (0, 0)),
            pl.BlockSpec((1, d, tn), lambda l, j: (l, 0, j)),
            pl.BlockSpec((1, 1, tn), lambda l, j: (l, 0, j)),
        ],
        out_specs=pl.BlockSpec((1, nb, tn), lambda l, j: (l, 0, j)),
        out_shape=jax.ShapeDtypeStruct((depth, nb, n), F32),
        compiler_params=_params("parallel", "parallel"),
        name="ada_mod",
    )(c, ada_w, ada_b.reshape(depth, 1, n))


def _modnorm(x, g, sc, sh):
    ms = jnp.mean(x * x, axis=-1, keepdims=True)
    return x * lax.rsqrt(ms + NORM_EPS) * g * (1.0 + sc) + sh


def _store_with_rope(y, o_ref, j, tn, ntiles, rope, tabs):
    if rope is None:
        o_ref[0] = y.astype(o_ref.dtype)
        return
    lo, hi, shift, stride, phase = rope
    c_ref, s1_ref, s2_ref = tabs
    cpt = tn // LANES

    def in_set(gc):
        return lo <= gc < hi and gc % stride == phase

    def put(c, roped):
        yc = y[:, c * LANES:(c + 1) * LANES]
        if roped:
            yc = _rope128(yc, c_ref[...], s1_ref[...], s2_ref[...], shift)
        o_ref[0, :, c * LANES:(c + 1) * LANES] = yc.astype(o_ref.dtype)

    by_tile = [[in_set(t * cpt + c) for t in range(ntiles)] for c in range(cpt)]
    dyn = []
    for c in range(cpt):
        if all(by_tile[c]) or not any(by_tile[c]):
            put(c, by_tile[c][0])
        else:
            dyn.append(c)
    if not dyn:
        return
    rope_tiles = [t for t in range(ntiles) if by_tile[dyn[0]][t]]
    whole_tiles = (all(by_tile[c] == by_tile[dyn[0]] for c in dyn)
                   and rope_tiles == list(range(rope_tiles[0], rope_tiles[-1] + 1)))
    if whole_tiles:
        is_rope = jnp.logical_and(j >= rope_tiles[0], j <= rope_tiles[-1])

        @pl.when(is_rope)
        def _():
            for c in dyn:
                put(c, True)

        @pl.when(jnp.logical_not(is_rope))
        def _():
            for c in dyn:
                put(c, False)
        return
    for c in dyn:
        gc = j * cpt + c
        is_rope = jnp.logical_and(jnp.logical_and(gc >= lo, gc < hi), gc % stride == phase)
        pl.when(is_rope)(functools.partial(put, c, True))
        pl.when(jnp.logical_not(is_rope))(functools.partial(put, c, False))


def _modmm_kernel(x_ref, g_ref, sc_ref, sh_ref, w_ref, *rest, tn, ntiles, rope):
    tabs, (o_ref, h_ref) = rest[:-2], rest[-2:]
    j = pl.program_id(2)

    @pl.when(j == 0)
    def _():
        h_ref[...] = _modnorm(x_ref[0], g_ref[...], sc_ref[0], sh_ref[0]).astype(BF16)

    y = jnp.dot(h_ref[...], w_ref[...], preferred_element_type=F32)
    _store_with_rope(y, o_ref, j, tn, ntiles, rope, tabs)


def modmm(x, g, scale, shift, w, *, ts, tn, out_dtype=BF16, rope=None, tabs=()):
    b, s, d = x.shape
    n = w.shape[1]
    w_spec = pl.BlockSpec((d, tn), lambda bi, i, j: (0, j),
                          pipeline_mode=pl.Buffered(1) if n == tn else None)
    tab_specs = [pl.BlockSpec((ts, LANES), lambda bi, i, j: (i, 0)) for _ in tabs]
    return pl.pallas_call(
        functools.partial(_modmm_kernel, tn=tn, ntiles=n // tn, rope=rope),
        grid=(b, s // ts, n // tn),
        in_specs=[
            pl.BlockSpec((1, ts, d), lambda bi, i, j: (bi, i, 0)),
            pl.BlockSpec((1, d), lambda bi, i, j: (0, 0)),
            pl.BlockSpec((1, 1, d), lambda bi, i, j: (bi, 0, 0)),
            pl.BlockSpec((1, 1, d), lambda bi, i, j: (bi, 0, 0)),
            w_spec,
        ] + tab_specs,
        out_specs=pl.BlockSpec((1, ts, tn), lambda bi, i, j: (bi, i, j)),
        out_shape=jax.ShapeDtypeStruct((b, s, n), out_dtype),
        scratch_shapes=[pltpu.VMEM((ts, d), BF16)],
        compiler_params=_params("parallel", "parallel", "arbitrary"),
        name="modmm",
    )(x, g, scale, shift, w, *tabs)


def _normmm_kernel(x_ref, g_ref, w_ref, *rest, tn, ntiles, rope):
    tabs, (o_ref, h_ref) = rest[:-2], rest[-2:]
    j = pl.program_id(2)

    @pl.when(j == 0)
    def _():
        x = x_ref[0].astype(F32)
        ms = jnp.mean(x * x, axis=-1, keepdims=True)
        h_ref[...] = (x * lax.rsqrt(ms + NORM_EPS) * g_ref[...]).astype(BF16)

    y = jnp.dot(h_ref[...], w_ref[...], preferred_element_type=F32)
    _store_with_rope(y, o_ref, j, tn, ntiles, rope, tabs)


def normmm(z, xcol, g, w, *, ts, tn, rope=None, tabs=()):
    b, s, _ = z.shape
    k, n = w.shape
    tab_specs = [pl.BlockSpec((ts, LANES), lambda bi, i, j: (i, 0)) for _ in tabs]
    return pl.pallas_call(
        functools.partial(_normmm_kernel, tn=tn, ntiles=n // tn, rope=rope),
        grid=(b, s // ts, n // tn),
        in_specs=[
            pl.BlockSpec((1, ts, k), lambda bi, i, j: (bi, i, xcol)),
            pl.BlockSpec((1, k), lambda bi, i, j: (0, 0)),
            pl.BlockSpec((k, tn), lambda bi, i, j: (0, j)),
        ] + tab_specs,
        out_specs=pl.BlockSpec((1, ts, tn), lambda bi, i, j: (bi, i, j)),
        out_shape=jax.ShapeDtypeStruct((b, s, n), BF16),
        scratch_shapes=[pltpu.VMEM((ts, k), BF16)],
        compiler_params=_params("parallel", "parallel", "arbitrary"),
        name="normmm",
    )(z, g, w, *tabs)


LOG2E = math.log2(math.e)


def _flash(q, k_ref, v_ref, m_sc, l_sc, acc_sc, tk):
    seq = k_ref.shape[0]
    m_sc[...] = jnp.full(m_sc.shape, -jnp.inf, F32)
    l_sc[...] = jnp.zeros(l_sc.shape, F32)
    acc_sc[...] = jnp.zeros(acc_sc.shape, F32)
    nck = tk // LANES
    nchunk = seq // tk

    def scores(c):
        return lax.dot_general(q, k_ref[c * tk:(c + 1) * tk, :], (((1,), (1,)), ((), ())),
                               preferred_element_type=F32)

    s_next = scores(0)
    for c in range(nchunk):
        s = s_next
        if c + 1 < nchunk:
            s_next = scores(c + 1)
        sc = [s[:, i * LANES:(i + 1) * LANES] for i in range(nck)]
        m_part = functools.reduce(jnp.maximum, sc)
        m_prev = m_sc[...]
        m_new = jnp.maximum(m_prev, jnp.max(m_part, axis=-1, keepdims=True))
        alpha = jnp.exp2(m_prev - m_new)
        pc = [jnp.exp2(x - m_new) for x in sc]
        l_sc[...] = alpha * l_sc[...] + functools.reduce(jnp.add, pc)
        p = jnp.concatenate([x.astype(BF16) for x in pc], axis=1)
        acc_sc[...] = alpha * acc_sc[...] + jnp.dot(
            p, v_ref[c * tk:(c + 1) * tk, :], preferred_element_type=F32)
        m_sc[...] = m_new


def _row_total(l):
    return jnp.sum(l, axis=-1, keepdims=True)


def _mla_attn_kernel(q_ref, kn_ref, v_ref, kpe_ref, o_ref, k_sc, m_sc, l_sc, acc_sc, *, tk, scale):
    @pl.when(pl.program_id(2) == 0)
    def _():
        k_sc[:, :LANES] = kn_ref[0]
        k_sc[:, LANES:] = kpe_ref[0]

    q = (q_ref[0].astype(F32) * (scale * LOG2E)).astype(BF16)
    _flash(q, k_sc, v_ref.at[0], m_sc, l_sc, acc_sc, tk)
    o_ref[0] = (acc_sc[...] / _row_total(l_sc[...])).astype(o_ref.dtype)


def mla_attention(q, kv, z, kpe_col, *, tq, tk):
    b, s, _ = q.shape
    scale = (MLA_NOPE + MLA_ROPE) ** -0.5
    return pl.pallas_call(
        functools.partial(_mla_attn_kernel, tk=tk, scale=scale),
        grid=(b, MLA_HEADS, s // tq),
        in_specs=[
            pl.BlockSpec((1, tq, MLA_QK_PAD), lambda bi, h, i: (bi, i, h)),
            pl.BlockSpec((1, s, LANES), lambda bi, h, i: (bi, 0, 2 * h)),
            pl.BlockSpec((1, s, LANES), lambda bi, h, i: (bi, 0, 2 * h + 1)),
            pl.BlockSpec((1, s, LANES), lambda bi, h, i: (bi, 0, kpe_col)),
        ],
        out_specs=pl.BlockSpec((1, tq, MLA_V), lambda bi, h, i: (bi, i, h)),
        out_shape=jax.ShapeDtypeStruct((b, s, MLA_HEADS * MLA_V), BF16),
        scratch_shapes=[
            pltpu.VMEM((s, MLA_QK_PAD), BF16),
            pltpu.VMEM((tq, LANES), F32),
            pltpu.VMEM((tq, LANES), F32),
            pltpu.VMEM((tq, MLA_V), F32),
        ],
        compiler_params=_params("parallel", "parallel", "arbitrary"),
        name="mla_attention",
    )(q, kv, kv, z)


def _diff_attn_kernel(q_ref, k_ref, v_ref, lam_ref, g_ref, o_ref, m_sc, l_sc, acc_sc,
                      *, tq, tk, scale, lam_init):
    q = q_ref[0].astype(F32) * (scale * LOG2E)
    lane = lax.broadcasted_iota(jnp.int32, q.shape, 1)
    q1 = jnp.where(lane < DIFF_HD, q, 0.0).astype(BF16)
    q2 = jnp.where(lane >= DIFF_HD, q, 0.0).astype(BF16)
    _flash(jnp.concatenate([q1, q2], axis=0), k_ref.at[0], v_ref.at[0], m_sc, l_sc, acc_sc, tk)
    lam = lam_ref[...]
    lam_full = (jnp.exp(jnp.sum(lam[0:1] * lam[1:2], axis=-1, keepdims=True))
                - jnp.exp(jnp.sum(lam[2:3] * lam[3:4], axis=-1, keepdims=True)) + lam_init)
    o1 = acc_sc[:tq, :] / _row_total(l_sc[:tq, :])
    o2 = acc_sc[tq:, :] / _row_total(l_sc[tq:, :])
    o = o1 - lam_full * o2
    ms = jnp.mean(o * o, axis=-1, keepdims=True)
    o = o * lax.rsqrt(ms + DIFF_SUBLN_EPS) * g_ref[...] * (1.0 - lam_init)
    o_ref[0] = o.astype(o_ref.dtype)


def diff_attention(z, lam, subln_g, lam_init, *, tq, tk):
    b, s, _ = z.shape
    scale = DIFF_HD ** -0.5
    nh = DIFF_HEADS
    return pl.pallas_call(
        functools.partial(_diff_attn_kernel, tq=tq, tk=tk, scale=scale, lam_init=lam_init),
        grid=(b, nh, s // tq),
        in_specs=[
            pl.BlockSpec((1, tq, LANES), lambda bi, h, i: (bi, i, h)),
            pl.BlockSpec((1, s, LANES), lambda bi, h, i: (bi, 0, nh + h)),
            pl.BlockSpec((1, s, LANES), lambda bi, h, i: (bi, 0, 2 * nh + h)),
            pl.BlockSpec((4, DIFF_HD), lambda bi, h, i: (0, 0)),
            pl.BlockSpec((1, LANES), lambda bi, h, i: (0, 0)),
        ],
        out_specs=pl.BlockSpec((1, tq, LANES), lambda bi, h, i: (bi, i, h)),
        out_shape=jax.ShapeDtypeStruct((b, s, nh * LANES), BF16),
        scratch_shapes=[
            pltpu.VMEM((2 * tq, LANES), F32),
            pltpu.VMEM((2 * tq, LANES), F32),
            pltpu.VMEM((2 * tq, LANES), F32),
        ],
        compiler_params=_params("parallel", "parallel", "arbitrary"),
        name="diff_attention",
    )(z, z, z, lam, subln_g.reshape(1, LANES))


def _finish(a, w_ref, x_ref, gm_ref, gp_ref, o_ref):
    y = jnp.dot(a.astype(BF16), w_ref[...], preferred_element_type=F32)
    ms = jnp.mean(y * y, axis=-1, keepdims=True)
    o_ref[0] = x_ref[0] + gm_ref[0] * (y * lax.rsqrt(ms + NORM_EPS) * gp_ref[...])


def _outmm_kernel(a_ref, g_ref, w_ref, x_ref, gm_ref, gp_ref, o_ref):
    a = a_ref[0].astype(F32) * _silu(g_ref[0].astype(F32))
    _finish(a, w_ref, x_ref, gm_ref, gp_ref, o_ref)


def _outmm2_kernel(a1_ref, a2_ref, g_ref, w_ref, x_ref, gm_ref, gp_ref, o_ref):
    a = (a1_ref[0] + a2_ref[0]) * _silu(g_ref[0].astype(F32))
    _finish(a, w_ref, x_ref, gm_ref, gp_ref, o_ref)


def outmm(acts, zgate, gcol, w, x, gate_mod, g_post, *, ts):
    b, s, d = x.shape
    row = lambda bi, i: (bi, i, 0)
    kern = _outmm_kernel if len(acts) == 1 else _outmm2_kernel
    return pl.pallas_call(
        kern,
        grid=(b, s // ts),
        in_specs=[pl.BlockSpec((1, ts, d), row) for _ in acts] + [
            pl.BlockSpec((1, ts, d), lambda bi, i: (bi, i, gcol)),
            pl.BlockSpec((d, d), lambda bi, i: (0, 0), pipeline_mode=pl.Buffered(1)),
            pl.BlockSpec((1, ts, d), row),
            pl.BlockSpec((1, 1, d), lambda bi, i: (bi, 0, 0)),
            pl.BlockSpec((1, d), lambda bi, i: (0, 0)),
        ],
        out_specs=pl.BlockSpec((1, ts, d), row),
        out_shape=jax.ShapeDtypeStruct((b, s, d), F32),
        compiler_params=_params("parallel", "parallel"),
        name="outmm",
    )(*acts, zgate, w, x, gate_mod, g_post)


def _lru_prepare(xe_sc, x_ref, prev_ref, next_ref, is_first, is_last, cw_ref, cb_ref, gw_ref,
                 gb_ref, sp_ref, a_sc, u_sc, d, ts, edge_row):
    w = LRU_WIDTH
    h = LRU_HALO
    xe_sc[pl.ds(h, ts), :] = x_ref[0].astype(F32)

    @pl.when(is_first)
    def _():
        xe_sc[pl.ds(0, h), :] = jnp.zeros((h, w), F32)

    @pl.when(jnp.logical_not(is_first))
    def _():
        xe_sc[pl.ds(0, h), :] = prev_ref[0].astype(F32)

    @pl.when(is_last)
    def _():
        xe_sc[pl.ds(h + ts, h), :] = jnp.zeros((h, w), F32)

    @pl.when(jnp.logical_not(is_last))
    def _():
        xe_sc[pl.ds(h + ts, h), :] = next_ref[0].astype(F32)

    xc = cb_ref[...] + sum(cw_ref[i:i + 1, :] * xe_sc[pl.ds(h - 1 + i, ts), :] for i in range(4))
    row = lax.broadcasted_iota(jnp.int32, (ts, LRU_BW), 0)
    for n in range(LRU_BLOCKS):
        cols = slice(n * LRU_BW, (n + 1) * LRU_BW)
        xn = xc[:, cols]
        g = jnp.dot(xn.astype(BF16), gw_ref[d, n], preferred_element_type=F32)
        r_t = jax.nn.sigmoid(g[:, :LRU_BW] + gb_ref[2 * d:2 * d + 1, cols])
        i_t = jax.nn.sigmoid(g[:, LRU_BW:] + gb_ref[2 * d + 1:2 * d + 2, cols])
        log_a = -LRU_C * r_t * sp_ref[d:d + 1, cols]
        a_t = jnp.exp(log_a)
        mult = jnp.sqrt(-jnp.tanh(log_a) * (1.0 + a_t * a_t))
        mult = jnp.where(row == edge_row, 1.0, mult)
        a_sc[:, cols] = a_t
        u_sc[:, cols] = mult * i_t * xn


def _lru_kernel(xf_ref, pf_ref, nf_ref, xb_ref, pb_ref, nb_ref, cw_ref, cb_ref, gw_ref, gb_ref,
                lam_ref, yf_ref, yb_ref, xe_sc, af_sc, uf_sc, ab_sc, ub_sc, hf_sc, hb_sc, sp_sc,
                *, ts):
    i = pl.program_id(1)
    nt = pl.num_programs(1)

    @pl.when(i == 0)
    def _():
        hf_sc[...] = jnp.zeros(hf_sc.shape, F32)
        hb_sc[...] = jnp.zeros(hb_sc.shape, F32)
        nl = -lam_ref[...]
        sp_sc[...] = jnp.maximum(nl, 0.0) + jnp.log(1.0 + jnp.exp(-jnp.abs(nl)))

    first, last = i == 0, i == nt - 1
    _lru_prepare(xe_sc, xf_ref, pf_ref, nf_ref, first, last, cw_ref, cb_ref, gw_ref, gb_ref,
                 sp_sc, af_sc, uf_sc, 0, ts, jnp.where(first, 0, -1))
    _lru_prepare(xe_sc, xb_ref, pb_ref, nb_ref, last, first, cw_ref, cb_ref, gw_ref, gb_ref,
                 sp_sc, ab_sc, ub_sc, 1, ts, jnp.where(first, ts - 1, -1))

    def step(k, carry):
        hf, hb = carry
        kb = ts - 1 - k
        hf = af_sc[pl.ds(k, 1), :] * hf + uf_sc[pl.ds(k, 1), :]
        hb = ab_sc[pl.ds(kb, 1), :] * hb + ub_sc[pl.ds(kb, 1), :]
        yf_ref[0, pl.ds(k, 1), :] = hf
        yb_ref[0, pl.ds(kb, 1), :] = hb
        return hf, hb

    hf, hb = lax.fori_loop(0, ts, step, (hf_sc[...], hb_sc[...]), unroll=8)
    hf_sc[...] = hf
    hb_sc[...] = hb


def lru_scan(z, conv_w, conv_b, gate_w, gate_b, lam, *, ts):
    b, s, _ = z.shape
    w = LRU_WIDTH
    h = LRU_HALO
    nt = s // ts
    rpb = ts // h
    nhb = s // h
    fwd = lambda bi, i: (bi, i, 0)
    bwd = lambda bi, i: (bi, nt - 1 - i, 0)
    fwd_prev = lambda bi, i: (bi, jnp.maximum(i * rpb - 1, 0), 0)
    fwd_next = lambda bi, i: (bi, jnp.minimum((i + 1) * rpb, nhb - 1), 0)
    bwd_prev = lambda bi, i: (bi, jnp.maximum((nt - 1 - i) * rpb - 1, 0), 0)
    bwd_next = lambda bi, i: (bi, jnp.minimum((nt - i) * rpb, nhb - 1), 0)
    full = lambda *shape: pl.BlockSpec(shape, lambda bi, i: (0,) * len(shape))
    return pl.pallas_call(
        functools.partial(_lru_kernel, ts=ts),
        grid=(b, nt),
        in_specs=[
            pl.BlockSpec((1, ts, w), fwd), pl.BlockSpec((1, h, w), fwd_prev),
            pl.BlockSpec((1, h, w), fwd_next),
            pl.BlockSpec((1, ts, w), bwd), pl.BlockSpec((1, h, w), bwd_prev),
            pl.BlockSpec((1, h, w), bwd_next),
            full(4, w), full(1, w), full(2, LRU_BLOCKS, LRU_BW, 2 * LRU_BW), full(4, w), full(2, w),
        ],
        out_specs=[pl.BlockSpec((1, ts, w), fwd), pl.BlockSpec((1, ts, w), bwd)],
        out_shape=[jax.ShapeDtypeStruct((b, s, w), F32)] * 2,
        scratch_shapes=[
            pltpu.VMEM((ts + 2 * h, w), F32),
            pltpu.VMEM((ts, w), F32), pltpu.VMEM((ts, w), F32),
            pltpu.VMEM((ts, w), F32), pltpu.VMEM((ts, w), F32),
            pltpu.VMEM((1, w), F32), pltpu.VMEM((1, w), F32), pltpu.VMEM((2, w), F32),
        ],
        compiler_params=_params("parallel", "arbitrary"),
        name="lru_scan",
    )(z, z, z, z, z, z, conv_w, conv_b, gate_w, gate_b, lam)


RWKV_HALO = 8


def _rwkv_shifted(x_ref, p_ref, n_ref, g_ref, sc_ref, sh_ref, he_sc, is_first, is_last, ts):
    hl = RWKV_HALO
    d = D_MODEL
    g, sc, sh = g_ref[...], sc_ref[0], sh_ref[0]
    he_sc[pl.ds(hl, ts), :] = _modnorm(x_ref[0], g, sc, sh)

    @pl.when(is_first)
    def _():
        he_sc[pl.ds(0, hl), :] = jnp.zeros((hl, d), F32)

    @pl.when(jnp.logical_not(is_first))
    def _():
        he_sc[pl.ds(0, hl), :] = _modnorm(p_ref[0], g, sc, sh)

    @pl.when(is_last)
    def _():
        he_sc[pl.ds(hl + ts, hl), :] = jnp.zeros((hl, d), F32)

    @pl.when(jnp.logical_not(is_last))
    def _():
        he_sc[pl.ds(hl + ts, hl), :] = _modnorm(n_ref[0], g, sc, sh)

    h = he_sc[pl.ds(hl, ts), :]
    xx = 0.5 * (he_sc[pl.ds(hl - 1, ts), :] + he_sc[pl.ds(hl + 1, ts), :]) - h
    return h, xx


def _rwkv_proj_kernel(x_ref, p_ref, n_ref, g_ref, sc_ref, sh_ref, mu_ref, w_ref, o_ref,
                      he_sc, xx_sc, *, ts):
    i, m = pl.program_id(1), pl.program_id(2)

    @pl.when(m == 0)
    def _():
        _, xx = _rwkv_shifted(x_ref, p_ref, n_ref, g_ref, sc_ref, sh_ref, he_sc,
                              i == 0, i == pl.num_programs(1) - 1, ts)
        xx_sc[...] = xx

    xs = he_sc[pl.ds(RWKV_HALO, ts), :] + xx_sc[...] * mu_ref[0]
    o_ref[0] = jnp.dot(xs.astype(BF16), w_ref[0], preferred_element_type=F32).astype(o_ref.dtype)


def _halo_specs(ts, s, halo, d):
    rpb, nhb = ts // halo, s // halo
    prev = lambda bi, i, *_: (bi, jnp.maximum(i * rpb - 1, 0), 0)
    nxt = lambda bi, i, *_: (bi, jnp.minimum((i + 1) * rpb, nhb - 1), 0)
    return pl.BlockSpec((1, halo, d), prev), pl.BlockSpec((1, halo, d), nxt)


def rwkv_proj(x, g, scale, shift, mu, w_in, *, ts):
    b, s, d = x.shape
    prev, nxt = _halo_specs(ts, s, RWKV_HALO, d)
    return pl.pallas_call(
        functools.partial(_rwkv_proj_kernel, ts=ts),
        grid=(b, s // ts, 4),
        in_specs=[
            pl.BlockSpec((1, ts, d), lambda bi, i, m: (bi, i, 0)), prev, nxt,
            pl.BlockSpec((1, d), lambda bi, i, m: (0, 0)),
            pl.BlockSpec((1, 1, d), lambda bi, i, m: (bi, 0, 0)),
            pl.BlockSpec((1, 1, d), lambda bi, i, m: (bi, 0, 0)),
            pl.BlockSpec((1, 1, d), lambda bi, i, m: (m, 0, 0)),
            pl.BlockSpec((1, d, d), lambda bi, i, m: (m, 0, 0)),
        ],
        out_specs=pl.BlockSpec((1, ts, d), lambda bi, i, m: (bi, i, m)),
        out_shape=jax.ShapeDtypeStruct((b, s, 4 * d), BF16),
        scratch_shapes=[pltpu.VMEM((ts + 2 * RWKV_HALO, d), F32), pltpu.VMEM((ts, d), F32)],
        compiler_params=_params("parallel", "parallel", "arbitrary"),
        name="rwkv_proj",
    )(x, x, x, g, scale, shift, mu.reshape(6, 1, d), w_in)


RWKV_DECAY_SCALE = math.exp(-0.5)


def _rwkv_lora_kernel(x_ref, p_ref, n_ref, g_ref, sc_ref, sh_ref, mu_ref, w0_ref, w1_ref, w2_ref,
                      a0_ref, a1_ref, a2_ref, lw_ref, a_ref, he_sc, *, ts):
    i = pl.program_id(1)
    h, xx = _rwkv_shifted(x_ref, p_ref, n_ref, g_ref, sc_ref, sh_ref, he_sc,
                          i == 0, i == pl.num_programs(1) - 1, ts)
    xw = (h + xx * mu_ref[4]).astype(BF16)
    xa = (h + xx * mu_ref[5]).astype(BF16)
    for dr in range(2):
        t1 = jnp.tanh(jnp.dot(xw, w1_ref[dr], preferred_element_type=F32))
        wl = w0_ref[dr] + jnp.dot(t1.astype(BF16), w2_ref[dr], preferred_element_type=F32)
        lw_ref[dr, 0] = -RWKV_DECAY_SCALE * jax.nn.sigmoid(wl)
        t2 = jnp.dot(xa, a1_ref[dr], preferred_element_type=F32)
        al = a0_ref[dr] + jnp.dot(t2.astype(BF16), a2_ref[dr], preferred_element_type=F32)
        a_ref[dr, 0] = jax.nn.sigmoid(al).astype(a_ref.dtype)


def rwkv_lora(x, g, scale, shift, mu, w0, w1, w2, a0, a1, a2, *, ts):
    b, s, d = x.shape
    lp = RWKV_LORA_PAD
    prev, nxt = _halo_specs(ts, s, RWKV_HALO, d)
    full = lambda *shape: pl.BlockSpec(shape, lambda bi, i: (0,) * len(shape))
    out = pl.BlockSpec((2, 1, ts, d), lambda bi, i: (0, bi, i, 0))
    return pl.pallas_call(
        functools.partial(_rwkv_lora_kernel, ts=ts),
        grid=(b, s // ts),
        in_specs=[
            pl.BlockSpec((1, ts, d), lambda bi, i: (bi, i, 0)), prev, nxt,
            full(1, d),
            pl.BlockSpec((1, 1, d), lambda bi, i: (bi, 0, 0)),
            pl.BlockSpec((1, 1, d), lambda bi, i: (bi, 0, 0)),
            full(6, 1, d), full(2, 1, d), full(2, d, lp), full(2, lp, d),
            full(2, 1, d), full(2, d, lp), full(2, lp, d),
        ],
        out_specs=[out, out],
        out_shape=[jax.ShapeDtypeStruct((2, b, s, d), F32), jax.ShapeDtypeStruct((2, b, s, d), BF16)],
        scratch_shapes=[pltpu.VMEM((ts + 2 * RWKV_HALO, d), F32)],
        compiler_params=_params("parallel", "parallel"),
        name="rwkv_lora",
    )(x, x, x, g, scale, shift, mu.reshape(6, 1, d), w0.reshape(2, 1, d), w1, w2,
      a0.reshape(2, 1, d), a1, a2)


def _hilo(x):
    hi = x.astype(BF16)
    return hi, (x - hi.astype(F32)).astype(BF16)


def _dot2(x, w):
    hi, lo = _hilo(x)
    return (jnp.dot(hi, w, preferred_element_type=F32) + jnp.dot(lo, w, preferred_element_type=F32))


def _nt(x, y):
    return lax.dot_general(x, y, (((1,), (1,)), ((), ())), preferred_element_type=F32)


def _bdot(x, y):
    return jnp.dot(x.astype(BF16), y.astype(BF16), preferred_element_type=F32)


def _rwkv_masks(reverse):
    t = RWKV_CHUNK
    n = RWKV_N
    lane_t = lax.broadcasted_iota(jnp.int32, (t, LANES), 1)
    row_t = lax.broadcasted_iota(jnp.int32, (t, LANES), 0)
    col_t = lane_t % n
    r128 = lax.broadcasted_iota(jnp.int32, (LANES, LANES), 0)
    c128 = lax.broadcasted_iota(jnp.int32, (LANES, LANES), 1)
    rt = lax.broadcasted_iota(jnp.int32, (t, t), 0)
    ct = lax.broadcasted_iota(jnp.int32, (t, t), 1)
    if reverse:
        strict2, incl2, tri = col_t > row_t, col_t >= row_t, ct >= rt
    else:
        strict2, incl2, tri = col_t < row_t, col_t <= row_t, ct <= rt
    same_head = (r128 // n) == (c128 // n)
    levels = []
    sz = 1
    while sz < t:
        hi_r, lo_c = (r128 % (2 * sz)) >= sz, (c128 % (2 * sz)) < sz
        if reverse:
            off = jnp.logical_and(jnp.logical_not(hi_r), jnp.logical_not(lo_c))
        else:
            off = jnp.logical_and(hi_r, lo_c)
        levels.append(jnp.logical_and(off, (r128 // (2 * sz)) == (c128 // (2 * sz))))
        sz *= 2
    return dict(m1=lane_t < n, strict2=strict2, incl2=incl2,
                tri=jnp.where(tri, 1.0, 0.0).astype(BF16), same_head=same_head,
                ones_bd=jnp.where(same_head, 1.0, 0.0).astype(BF16),
                eye=jnp.where(r128 == c128, 1.0, 0.0), levels=levels)


def _each(f, *lists):
    return [f(*xs) for xs in zip(*lists)]


def _rwkv_chunk_terms(lw, r, k, v, a, kk_scale, ka_scale, rk_scale, mk, reverse):
    t = RWKV_CHUNK
    m1, strict2, incl2, tri, same_head = mk['m1'], mk['strict2'], mk['incl2'], mk['tri'], mk['same_head']

    def sel(top, bot):
        return jnp.where(m1, top, bot)

    def stack(x):
        return jnp.concatenate([jnp.where(m1, x, 0.0), jnp.where(m1, 0.0, x)], axis=0)

    kk = _each(jnp.multiply, k, kk_scale)
    kd = _each(lambda k_, a_, s_: k_ * (1.0 + (a_ - 1.0) * s_), k, a, ka_scale)
    seg = _each(lambda kk_, r_, kd_, s_: _bdot(
        jnp.concatenate([kk_ * kk_, r_ * kd_ * s_], axis=0), mk['ones_bd']), kk, r, kd, rk_scale)
    kk = _each(lambda kk_, sg: kk_ / jnp.maximum(jnp.sqrt(sg[:t]), 1e-12), kk, seg)
    bonus = _each(lambda sg, v_: sg[t:] * v_, seg, v)

    c = _each(lambda lw_: _dot2_left(tri, lw_), lw)
    e_pos = _each(jnp.exp, c)
    e_neg = _each(lambda c_: jnp.exp(-c_), c)
    at = _each(lambda kk_, c_, lw_: -kk_ * jnp.exp(c_ - lw_), kk, c, lw)
    rt_ = _each(jnp.multiply, r, e_pos)
    bt = _each(lambda kk_, a_, e_: kk_ * a_ * e_, kk, a, e_neg)
    kt = _each(jnp.multiply, kd, e_neg)
    big = _each(lambda at_, rt__, bt_, kt_: _nt(
        jnp.concatenate([at_, rt__], axis=0).astype(BF16),
        jnp.concatenate([stack(bt_), stack(kt_)], axis=0).astype(BF16)), at, rt_, bt, kt)
    aak = _each(lambda b_: jnp.where(strict2, b_[:t, LANES:], 0.0), big)
    rb = _each(lambda b_: jnp.where(incl2, b_[t:, :LANES], 0.0), big)
    rk = _each(lambda b_: jnp.where(incl2, b_[t:, LANES:], 0.0), big)

    abd = _each(lambda b_: stack(jnp.where(strict2, b_[:t, :LANES], 0.0)), big)
    dinv = _each(lambda ab: mk['eye'] + jnp.where(mk['levels'][0], ab, 0.0), abd)
    for off in mk['levels'][1:]:
        dinv_b = _each(lambda d_: d_.astype(BF16), dinv)
        half = _each(lambda db, ab: jnp.dot(db, jnp.where(off, ab, 0.0).astype(BF16),
                                            preferred_element_type=F32), dinv_b, abd)
        dinv = _each(lambda d_, h_, db: d_ + jnp.dot(h_.astype(BF16), db, preferred_element_type=F32),
                     dinv, half, dinv_b)

    vst = _each(stack, v)
    aakv = _each(_bdot, aak, vst)
    x = _each(lambda d_, at_, av: _bdot(d_, jnp.concatenate(
        [jnp.concatenate([at_, av], axis=1)] * 2, axis=0)), dinv, at, aakv)
    atp = _each(lambda x_: sel(x_[:t, :LANES], x_[t:, :LANES]), x)
    u0 = _each(lambda x_: sel(x_[:t, LANES:], x_[t:, LANES:]), x)
    y = _each(lambda rb_, p_, u_: _bdot(rb_, jnp.concatenate([stack(p_), stack(u_)], axis=1)),
              rb, atp, u0)
    rkv = _each(_bdot, rk, vst)
    rp = _each(lambda rt__, y_: rt__ + y_[:, :LANES], rt_, y)
    o0 = _each(lambda y_, q_: y_[:, LANES:] + q_, y, rkv)
    gmt = _each(lambda p_, bt_: jnp.where(same_head, _bdot(p_.T, bt_), 0.0), atp, bt)
    zty = _each(lambda u_, v_, bt_, kt_: jnp.where(same_head, _bdot(
        jnp.concatenate([u_, v_], axis=0).T, jnp.concatenate([bt_, kt_], axis=0)), 0.0),
        u0, v, bt, kt)
    decay_row = _each(lambda e_: e_[0:1, :] if reverse else e_[t - 1:t, :], e_pos)
    return list(zip(rp, o0, gmt, zty, decay_row, bonus))


def _dot2_left(w, x):
    hi, lo = _hilo(x)
    return (jnp.dot(w, hi, preferred_element_type=F32) + jnp.dot(w, lo, preferred_element_type=F32))


def _rwkv_scan_kernel(lw_ref, a_ref, r_ref, k_ref, v_ref, kk_ref, ka_ref, rk_ref, o_ref, bo_ref,
                      s_sc, *, ts, npairs, reverse):
    @pl.when(pl.program_id(2) == 0)
    def _():
        s_sc[...] = jnp.zeros(s_sc.shape, F32)

    t = RWKV_CHUNK
    nch = ts // t
    order = range(nch - 1, -1, -1) if reverse else range(nch)
    mk = _rwkv_masks(reverse)
    chains = [(ci, hp) for ci in order for hp in range(npairs)]
    rows = lambda ci: slice(ci * t, (ci + 1) * t)
    cols = lambda hp: slice(hp * LANES, (hp + 1) * LANES)
    blk = lambda ref: [ref[0, rows(ci), cols(hp)].astype(F32) for ci, hp in chains]
    blk4 = lambda ref: [ref[0, 0, rows(ci), cols(hp)].astype(F32) for ci, hp in chains]
    par = lambda ref: [ref[:, cols(hp)] for _, hp in chains]
    terms = dict(zip(chains, _rwkv_chunk_terms(
        blk4(lw_ref), blk(r_ref), blk(k_ref), blk(v_ref), blk4(a_ref),
        par(kk_ref), par(ka_ref), par(rk_ref), mk, reverse)))
    s = [s_sc[hp] for hp in range(npairs)]
    for ci in order:
        for hp in range(npairs):
            rp, o0, gmt, zty, decay_row, bonus = terms[ci, hp]
            s_b = s[hp].astype(BF16)
            o_ref[0, rows(ci), cols(hp)] = o0 + _nt(rp.astype(BF16), s_b)
            bo_ref[0, rows(ci), cols(hp)] = bonus.astype(bo_ref.dtype)
            s[hp] = (s[hp] + jnp.dot(s_b, gmt.astype(BF16), preferred_element_type=F32)
                     + zty) * decay_row
    for hp in range(npairs):
        s_sc[hp] = s[hp]


def rwkv_scan(lw, a, rkvg, k_k, k_a, r_k, dr, *, ts, npairs):
    _, b, s, d = lw.shape
    nt = s // ts
    wd = npairs * LANES
    ng = d // wd
    reverse = dr == 1
    tmap = (lambda i: nt - 1 - i) if reverse else (lambda i: i)
    dspec = pl.BlockSpec((1, 1, ts, wd), lambda bi, g, i: (dr, bi, tmap(i), g))
    col = lambda m: pl.BlockSpec((1, ts, wd), lambda bi, g, i: (bi, tmap(i), m * ng + g))
    par = pl.BlockSpec((1, wd), lambda bi, g, i: (0, g))
    out = pl.BlockSpec((1, ts, wd), lambda bi, g, i: (bi, tmap(i), g))
    return pl.pallas_call(
        functools.partial(_rwkv_scan_kernel, ts=ts, npairs=npairs, reverse=reverse),
        grid=(b, ng, nt),
        in_specs=[dspec, dspec, col(0), col(1), col(2), par, par, par],
        out_specs=[out, out],
        out_shape=[jax.ShapeDtypeStruct((b, s, d), F32), jax.ShapeDtypeStruct((b, s, d), BF16)],
        scratch_shapes=[pltpu.VMEM((npairs, LANES, LANES), F32)],
        compiler_params=_params("parallel", "parallel", "arbitrary"),
        name="rwkv_scan_bwd" if reverse else "rwkv_scan_fwd",
    )(lw, a, rkvg, rkvg, rkvg, k_k, k_a, r_k)


def _rwkv_out_kernel(of_ref, ob_ref, bf_ref, bb_ref, g_ref, lg_ref, lb_ref, w_ref, x_ref, gm_ref,
                     gp_ref, o_ref, a_sc):
    r128 = lax.broadcasted_iota(jnp.int32, (LANES, LANES), 0)
    c128 = lax.broadcasted_iota(jnp.int32, (LANES, LANES), 1)
    avg = jnp.where((r128 // RWKV_N) == (c128 // RWKV_N), 1.0 / RWKV_N, 0.0).astype(BF16)
    for c in range(D_MODEL // LANES):
        cols = slice(c * LANES, (c + 1) * LANES)
        o = of_ref[0, :, cols] + ob_ref[0, :, cols]
        cen = o - _dot2(o, avg)
        var = _dot2(cen * cen, avg)
        on = cen * lax.rsqrt(var + RWKV_GN_EPS)
        y = (on * lg_ref[:, cols] + lb_ref[:, cols]
             + bf_ref[0, :, cols].astype(F32) + bb_ref[0, :, cols].astype(F32))
        a_sc[:, cols] = (y * _silu(g_ref[0, :, cols].astype(F32))).astype(BF16)
    _finish(a_sc[...], w_ref, x_ref, gm_ref, gp_ref, o_ref)


def rwkv_out(o_f, o_b, bo_f, bo_b, rkvg, ln_g, ln_b, w, x, gate_mod, g_post, *, ts):
    b, s, d = x.shape
    row = lambda bi, i: (bi, i, 0)
    vec = pl.BlockSpec((1, d), lambda bi, i: (0, 0))
    return pl.pallas_call(
        _rwkv_out_kernel,
        grid=(b, s // ts),
        in_specs=[pl.BlockSpec((1, ts, d), row)] * 4 + [
            pl.BlockSpec((1, ts, d), lambda bi, i: (bi, i, 3)), vec, vec,
            pl.BlockSpec((d, d), lambda bi, i: (0, 0)),
            pl.BlockSpec((1, ts, d), row),
            pl.BlockSpec((1, 1, d), lambda bi, i: (bi, 0, 0)),
            vec,
        ],
        out_specs=pl.BlockSpec((1, ts, d), row),
        out_shape=jax.ShapeDtypeStruct((b, s, d), F32),
        scratch_shapes=[pltpu.VMEM((ts, d), BF16)],
        compiler_params=_params("parallel", "parallel"),
        name="rwkv_out",
    )(o_f, o_b, bo_f, bo_b, rkvg, ln_g, ln_b, w, x, gate_mod, g_post)


def _prep_weights(p):
    d = D_MODEL
    w = {}
    wi = p['mla_w_in'][0]
    q_lat, kv_lat, k_pe, gate = jnp.split(
        wi, [MLA_Q_RANK, MLA_Q_RANK + MLA_KV_RANK, MLA_Q_RANK + MLA_KV_RANK + MLA_ROPE], axis=1)
    w['mla_in'] = jnp.concatenate(
        [gate, q_lat, kv_lat, k_pe, jnp.zeros((d, LANES - MLA_ROPE), F32)], axis=1).astype(BF16)
    wq = p['mla_w_q_up'][0].reshape(MLA_Q_RANK, MLA_HEADS, MLA_NOPE + MLA_ROPE)
    wq = jnp.pad(wq, ((0, 0), (0, 0), (0, MLA_QK_PAD - MLA_NOPE - MLA_ROPE)))
    w['mla_q'] = wq.reshape(MLA_Q_RANK, MLA_HEADS * MLA_QK_PAD).astype(BF16)
    w['mla_kv'] = p['mla_w_kv_up'][0].astype(BF16)
    w['mla_out'] = p['mla_w_out'][0].astype(BF16)
    w['diff_in'] = p['diff_w_in'][0].astype(BF16)
    w['diff_out'] = p['diff_w_out'][0].astype(BF16)
    w['lru_in'] = p['lru_w_in'][0].astype(BF16)
    w['lru_out'] = p['lru_w_out'][0].astype(BF16)
    gw = p['lru_gate_w'][0]
    w['lru_gate'] = jnp.concatenate([gw[:, 0], gw[:, 1]], axis=-1).astype(BF16)
    w['rwkv_in'] = p['rwkv_w_in'][0].astype(BF16)
    pad = RWKV_LORA_PAD - RWKV_LORA
    for nm in ('w1', 'a1'):
        w['rwkv_' + nm] = jnp.pad(p['rwkv_' + nm][0], ((0, 0), (0, 0), (0, pad))).astype(BF16)
    for nm in ('w2', 'a2'):
        w['rwkv_' + nm] = jnp.pad(p['rwkv_' + nm][0], ((0, 0), (0, pad), (0, 0))).astype(BF16)
    w['rwkv_out'] = p['rwkv_w_out'][0].astype(BF16)
    return w


def _mla_layer(x, pre_g, scale, shift, gate_mod, post_g, p, w, tabs):
    mla_tabs = tabs['mla']
    z = modmm(x, pre_g, scale, shift, w['mla_in'], ts=512, tn=3200,
              rope=(24, 25, MLA_ROPE // 2, 1, 0), tabs=mla_tabs)
    q = normmm(z, 4, p['mla_q_norm_g'], w['mla_q'], ts=1024, tn=512,
               rope=(0, 2 * MLA_HEADS, MLA_ROPE // 2, 2, 1), tabs=mla_tabs)
    kv = normmm(z, 5, p['mla_kv_norm_g'], w['mla_kv'], ts=1024, tn=512)
    o = mla_attention(q, kv, z, 24, tq=1024, tk=1024)
    return outmm([o], z, 0, w['mla_out'], x, gate_mod, post_g, ts=512)


def _diff_layer(x, layer_idx, pre_g, scale, shift, gate_mod, post_g, p, w, tabs):
    z = modmm(x, pre_g, scale, shift, w['diff_in'], ts=1024, tn=1024,
              rope=(0, 2 * DIFF_HEADS, DIFF_ROT // 2, 1, 0), tabs=tabs['diff'])
    lam_init = 0.8 - 0.6 * math.exp(-0.3 * layer_idx)
    o = diff_attention(z, p['diff_lambda'][0], p['diff_subln_g'][0], lam_init, tq=512, tk=1024)
    return outmm([o], z, 3, w['diff_out'], x, gate_mod, post_g, ts=512)


def _lru_layer(x, pre_g, scale, shift, gate_mod, post_g, p, w):
    z = modmm(x, pre_g, scale, shift, w['lru_in'], ts=1024, tn=1024)
    y_f, y_b = lru_scan(z, p['lru_conv_w'][0], p['lru_conv_b'], w['lru_gate'],
                        p['lru_gate_b'][0].reshape(4, LRU_WIDTH), p['lru_lambda'][0], ts=256)
    return outmm([y_f, y_b], z, 1, w['lru_out'], x, gate_mod, post_g, ts=256)


def _rwkv_layer(x, pre_g, scale, shift, gate_mod, post_g, p, w):
    mu = p['rwkv_mu'][0]
    rkvg = rwkv_proj(x, pre_g, scale, shift, mu, w['rwkv_in'], ts=512)
    lw, a = rwkv_lora(x, pre_g, scale, shift, mu, p['rwkv_w0'][0], w['rwkv_w1'], w['rwkv_w2'],
                      p['rwkv_a0'][0], w['rwkv_a1'], w['rwkv_a2'], ts=256)
    k_k = p['rwkv_k_k']
    k_a = p['rwkv_k_a']
    r_k = p['rwkv_r_k'].reshape(1, D_MODEL)
    o_f, bo_f = rwkv_scan(lw, a, rkvg, k_k, k_a, r_k, 0, ts=256, npairs=4)
    o_b, bo_b = rwkv_scan(lw, a, rkvg, k_k, k_a, r_k, 1, ts=256, npairs=4)
    return rwkv_out(o_f, o_b, bo_f, bo_b, rkvg, p['rwkv_ln_g'], p['rwkv_ln_b'], w['rwkv_out'],
                    x, gate_mod, post_g, ts=256)


def _trunk(x, mods, p, w):
    b, s, d = x.shape
    tabs = {
        'mla': _rope_tables(s, 1, LANES, MLA_ROPE, MLA_THETA, 0.0),
        'diff': _rope_tables(s, LANES // DIFF_HD, DIFF_HD, DIFF_ROT, ROPE_THETA, 1.0),
    }
    for i in range(DEPTH):
        shift, scale, gate_mod = (mods[i, :, k * d:(k + 1) * d].reshape(b, 1, d) for k in range(3))
        pre_g = p['norm_pre_g'][i:i + 1]
        post_g = p['norm_post_g'][i:i + 1]
        args = (pre_g, scale, shift, gate_mod, post_g, p, w)
        if i % 4 == 0:
            x = _mla_layer(x, *args, tabs)
        elif i % 4 == 1:
            x = _diff_layer(x, i, *args, tabs)
        elif i % 4 == 2:
            x = _lru_layer(x, *args)
        else:
            x = _rwkv_layer(x, *args)
    return x


def kernel(x_prompt, x_sample, c_prompt, c_sample, ada_w, ada_b, norm_pre_g, norm_post_g, mla_w_in, mla_q_norm_g, mla_kv_norm_g, mla_w_q_up, mla_w_kv_up, mla_w_out, diff_w_in, diff_lambda, diff_subln_g, diff_w_out, lru_w_in, lru_conv_w, lru_conv_b, lru_gate_w, lru_gate_b, lru_lambda, lru_w_out, rwkv_mu, rwkv_w_in, rwkv_w0, rwkv_w1, rwkv_w2, rwkv_a0, rwkv_a1, rwkv_a2, rwkv_k_k, rwkv_k_a, rwkv_r_k, rwkv_ln_g, rwkv_ln_b, rwkv_w_out):
    p = dict(
        norm_pre_g=norm_pre_g, norm_post_g=norm_post_g,
        mla_w_in=mla_w_in, mla_q_norm_g=mla_q_norm_g, mla_kv_norm_g=mla_kv_norm_g,
        mla_w_q_up=mla_w_q_up, mla_w_kv_up=mla_w_kv_up, mla_w_out=mla_w_out,
        diff_w_in=diff_w_in, diff_lambda=diff_lambda, diff_subln_g=diff_subln_g,
        diff_w_out=diff_w_out,
        lru_w_in=lru_w_in, lru_conv_w=lru_conv_w, lru_conv_b=lru_conv_b, lru_gate_w=lru_gate_w,
        lru_gate_b=lru_gate_b, lru_lambda=lru_lambda, lru_w_out=lru_w_out,
        rwkv_mu=rwkv_mu, rwkv_w_in=rwkv_w_in, rwkv_w0=rwkv_w0, rwkv_w1=rwkv_w1, rwkv_w2=rwkv_w2,
        rwkv_a0=rwkv_a0, rwkv_a1=rwkv_a1, rwkv_a2=rwkv_a2, rwkv_k_k=rwkv_k_k, rwkv_k_a=rwkv_k_a,
        rwkv_r_k=rwkv_r_k, rwkv_ln_g=rwkv_ln_g, rwkv_ln_b=rwkv_ln_b, rwkv_w_out=rwkv_w_out,
    )
    w = _prep_weights(p)
    nb = x_prompt.shape[0]
    mods = ada_mod(jnp.concatenate([c_prompt, c_sample], axis=0), ada_w, ada_b)
    y_prompt = _trunk(x_prompt, mods[:, :nb], p, w)
    y_sample = _trunk(x_sample, mods[:, nb:], p, w)
    return (y_prompt, y_sample)
```
